```python
import jax, jax.numpy as jnp
from jax import lax
import numpy as np

D_MODEL = 1024
BATCH = 2
SEQ = 8192
DEPTH = 1

CHUNK = 64
RNN_WIDTH = 1024
RNN_HEADS = 8
RNN_HEAD_DIM = RNN_WIDTH // RNN_HEADS
CONV_WIDTH = 4
LRU_C = 8.0
SGU_WIDTH = 1024
SGU_GROUPS = 8
SGU_GROUP_DIM = SGU_WIDTH // SGU_GROUPS
SGU_BLOCK = 2 * CHUNK
D_FF = 3 * D_MODEL
FFN_CONV_WIDTH = 3
N_BRANCHES = 2
N_MOD = 6
EPS = 1e-6
IN_COLS = 2 * RNN_WIDTH + 2 * SGU_WIDTH + N_BRANCHES * D_MODEL

kernel_name = "hybrid_rglru_sgu_convffn_block"


def _rmsnorm(x, g):
    x32 = x.astype(jnp.float32)
    y = x32 * lax.rsqrt(jnp.mean(x32 * x32, axis=-1, keepdims=True) + EPS)
    return (y * g.astype(jnp.float32)).astype(x.dtype)


def _layernorm(x, g, b):
    x32 = x.astype(jnp.float32)
    mu = jnp.mean(x32, axis=-1, keepdims=True)
    var = jnp.mean(jnp.square(x32 - mu), axis=-1, keepdims=True)
    y = (x32 - mu) * lax.rsqrt(var + EPS)
    return (y * g.astype(jnp.float32) + b.astype(jnp.float32)).astype(x.dtype)


def _modulated_norm(x, g, shift, scale):
    return _rmsnorm(x, g) * (1.0 + scale[:, None, :]) + shift[:, None, :]


def _causal_dwconv(x, w, b):
    k_width = w.shape[0]
    seq = x.shape[1]
    xp = jnp.pad(x, ((0, 0), (k_width - 1, 0), (0, 0)))
    y = b + xp[:, 0:seq, :] * w[0]
    for k in range(1, k_width):
        y = y + xp[:, k:k + seq, :] * w[k]
    return y


def _block_diag(x, w, b):
    bsz, seq, _ = x.shape
    xh = x.reshape(bsz, seq, RNN_HEADS, RNN_HEAD_DIM)
    y = jnp.einsum("bshi,hij->bshj", xh, w)
    return y.reshape(bsz, seq, RNN_WIDTH) + b


def _lin_combine(left, right):
    a_l, u_l = left
    a_r, u_r = right
    return a_l * a_r, a_r * u_l + u_r


def _rg_lru(x, w_a, b_a, w_x, b_x, lam):
    r = jax.nn.sigmoid(_block_diag(x, w_a, b_a).astype(jnp.float32))
    i = jax.nn.sigmoid(_block_diag(x, w_x, b_x).astype(jnp.float32))
    log_a = LRU_C * r * jax.nn.log_sigmoid(lam.astype(jnp.float32))
    a = jnp.exp(log_a)
    mult = jnp.sqrt(-jnp.expm1(2.0 * log_a))
    u = mult * (i * x.astype(jnp.float32))
    _, h = lax.associative_scan(_lin_combine, (a, u), axis=1)
    return h.astype(x.dtype)


def _spatial_gating(u, v, ln_g, ln_b, w_s, b_s):
    bsz, seq, _ = v.shape
    n_blk = seq // SGU_BLOCK
    v = _layernorm(v, ln_g, ln_b)
    vb = v.reshape(bsz, n_blk, SGU_BLOCK, SGU_GROUPS, SGU_GROUP_DIM)
    mask = jnp.tril(jnp.ones((SGU_BLOCK, SGU_BLOCK), dtype=w_s.dtype))
    mixed = jnp.einsum("gts,bnsgd->bntgd", w_s * mask, vb)
    mixed = mixed + jnp.transpose(b_s)[None, None, :, :, None]
    return u * mixed.reshape(bsz, seq, SGU_WIDTH)


def setup_inputs(seed: int = 0) -> dict:
    key = jax.random.key(seed)
    ks = jax.random.split(key, 26)

    def nrm(k, shape, scale):
        return jax.random.normal(k, shape, jnp.float32) * scale

    a_c = jax.random.uniform(ks[12], (DEPTH, RNN_WIDTH), jnp.float32, 0.9, 0.999)
    s = a_c ** (1.0 / LRU_C)
    lru_lambda = jnp.log(s) - jnp.log1p(-s)

    return {
        "x": nrm(ks[0], (BATCH, SEQ, D_MODEL), 1.0),
        "c": nrm(ks[1], (BATCH, D_MODEL), 1.0),
        "w_ada": nrm(ks[2], (DEPTH, D_MODEL, N_MOD * D_MODEL), D_MODEL ** -0.5),
        "b_ada": nrm(ks[3], (DEPTH, N_MOD * D_MODEL), 0.02),
        "norm_mix_g": 1.0 + nrm(ks[4], (DEPTH, D_MODEL), 0.05),
        "w_in": nrm(ks[5], (DEPTH, D_MODEL, IN_COLS), D_MODEL ** -0.5),
        "rnn_conv_w": nrm(ks[6], (DEPTH, CONV_WIDTH, RNN_WIDTH), CONV_WIDTH ** -0.5),
        "rnn_conv_b": nrm(ks[7], (DEPTH, RNN_WIDTH), 0.02),
        "lru_w_a": nrm(ks[8], (DEPTH, RNN_HEADS, RNN_HEAD_DIM, RNN_HEAD_DIM), RNN_HEAD_DIM ** -0.5),
        "lru_b_a": nrm(ks[9], (DEPTH, RNN_WIDTH), 0.02),
        "lru_w_x": nrm(ks[10], (DEPTH, RNN_HEADS, RNN_HEAD_DIM, RNN_HEAD_DIM), RNN_HEAD_DIM ** -0.5),
        "lru_b_x": nrm(ks[11], (DEPTH, RNN_WIDTH), 0.02),
        "lru_lambda": lru_lambda,
        "sgu_ln_g": 1.0 + nrm(ks[13], (DEPTH, SGU_WIDTH), 0.05),
        "sgu_ln_b": nrm(ks[14], (DEPTH, SGU_WIDTH), 0.02),
        "sgu_w_s": nrm(ks[15], (DEPTH, SGU_GROUPS, SGU_BLOCK, SGU_BLOCK), SGU_BLOCK ** -0.5),
        "sgu_b_s": 1.0 + nrm(ks[16], (DEPTH, SGU_GROUPS, SGU_BLOCK), 0.1),
        "w_branch_a": nrm(ks[17], (DEPTH, RNN_WIDTH, D_MODEL), RNN_WIDTH ** -0.5),
        "w_branch_b": nrm(ks[18], (DEPTH, SGU_WIDTH, D_MODEL), SGU_WIDTH ** -0.5),
        "w_out": nrm(ks[19], (DEPTH, D_MODEL, D_MODEL), D_MODEL ** -0.5),
        "norm_ffn_g": 1.0 + nrm(ks[20], (DEPTH, D_MODEL), 0.05),
        "w_up": nrm(ks[21], (DEPTH, D_MODEL, 2 * D_FF), D_MODEL ** -0.5),
        "ffn_conv_w": nrm(ks[22], (DEPTH, FFN_CONV_WIDTH, 2 * D_FF), FFN_CONV_WIDTH ** -0.5),
        "ffn_conv_b": nrm(ks[23], (DEPTH, 2 * D_FF), 0.02),
        "w_down": nrm(ks[24], (DEPTH, D_FF, D_MODEL), D_FF ** -0.5),
        "norm_final_g": 1.0 + nrm(ks[25], (D_MODEL,), 0.05),
    }


def reference(x, c, w_ada, b_ada, norm_mix_g, w_in, rnn_conv_w, rnn_conv_b,
              lru_w_a, lru_b_a, lru_w_x, lru_b_x, lru_lambda,
              sgu_ln_g, sgu_ln_b, sgu_w_s, sgu_b_s,
              w_branch_a, w_branch_b, w_out,
              norm_ffn_g, w_up, ffn_conv_w, ffn_conv_b, w_down, norm_final_g):
    split_idx = [RNN_WIDTH, 2 * RNN_WIDTH, 2 * RNN_WIDTH + SGU_WIDTH,
                 2 * RNN_WIDTH + 2 * SGU_WIDTH, 2 * RNN_WIDTH + 2 * SGU_WIDTH + D_MODEL]
    c_act = jax.nn.silu(c)
    for l in range(DEPTH):
        mod = c_act @ w_ada[l] + b_ada[l]
        shift1, scale1, gate1, shift2, scale2, gate2 = jnp.split(mod, N_MOD, axis=-1)

        h = _modulated_norm(x, norm_mix_g[l], shift1, scale1)
        z = h @ w_in[l]
        xr, gr, zu, zv, ga, gb = jnp.split(z, split_idx, axis=-1)

        xr = _causal_dwconv(xr, rnn_conv_w[l], rnn_conv_b[l])
        y_a = _rg_lru(xr, lru_w_a[l], lru_b_a[l], lru_w_x[l], lru_b_x[l], lru_lambda[l])
        y_a = (y_a * jax.nn.gelu(gr)) @ w_branch_a[l]

        y_b = _spatial_gating(jax.nn.gelu(zu), jax.nn.gelu(zv), sgu_ln_g[l], sgu_ln_b[l],
                              sgu_w_s[l], sgu_b_s[l])
        y_b = y_b @ w_branch_b[l]

        merged = jax.nn.sigmoid(ga) * y_a + jax.nn.sigmoid(gb) * y_b
        x = x + gate1[:, None, :] * (merged @ w_out[l])

        h = _modulated_norm(x, norm_ffn_g[l], shift2, scale2)
        hid = _causal_dwconv(h @ w_up[l], ffn_conv_w[l], ffn_conv_b[l])
        act, val = jnp.split(hid, 2, axis=-1)
        x = x + gate2[:, None, :] * ((jax.nn.gelu(act) * val) @ w_down[l])

    return _rmsnorm(x, norm_final_g)
```

```python
import functools

import jax
import jax.numpy as jnp
from jax.experimental import pallas as pl
from jax.experimental.pallas import tpu as pltpu

EPS = 1e-6
LRU_C = 8.0
RNN_HEADS = 8
SGU_GROUPS = 8
SGU_BLOCK = 128
N_MOD = 6

V7X_SUBLANES = 8
TIME_TILE = 256
ADALN_COL_BLOCK = 1536
VMEM_LIMIT_BYTES = 56 * 1024 * 1024

_GELU_C0 = 0.7978845608028654
_GELU_C1 = 0.044715


def _gelu(x):
    inner = x * (_GELU_C0 + (_GELU_C0 * _GELU_C1) * (x * x))
    hx = 0.5 * x
    return hx + hx * jnp.tanh(inner)


def _sigmoid(x):
    return 0.5 * jnp.tanh(0.5 * x) + 0.5


def _bdot(a, b):
    return jnp.dot(a, b, preferred_element_type=jnp.float32)


def _mod_norm(x, gs, shift):
    r = jax.lax.rsqrt(jnp.mean(x * x, axis=-1, keepdims=True) + EPS)
    return (x * r) * gs + shift


def _adaln_kernel(ct_ref, w_ref, b_ref, o_ref):
    ct = ct_ref[...]
    act = ct * _sigmoid(ct)
    w = w_ref[...]
    rows = []
    for b in range(ct.shape[1]):
        rows.append(jnp.sum(act[:, b:b + 1] * w, axis=0, keepdims=True))
    o_ref[...] = jnp.concatenate(rows, axis=0) + b_ref[...]


def _adaln(c, w_ada, b_ada):
    bsz, d = c.shape
    n = w_ada.shape[1]
    bn = ADALN_COL_BLOCK
    return pl.pallas_call(
        _adaln_kernel,
        grid=(n // bn,),
        in_specs=[
            pl.BlockSpec((d, bsz), lambda j: (0, 0)),
            pl.BlockSpec((d, bn), lambda j: (0, j)),
            pl.BlockSpec((1, bn), lambda j: (0, j)),
        ],
        out_specs=pl.BlockSpec((bsz, bn), lambda j: (0, j)),
        out_shape=jax.ShapeDtypeStruct((bsz, n), jnp.float32),
        compiler_params=pltpu.CompilerParams(
            dimension_semantics=("arbitrary",), vmem_limit_bytes=VMEM_LIMIT_BYTES),
        name="adaln",
    )(c.T, w_ada, b_ada.reshape(1, n))


def _shift_rows(x, k):
    return jnp.concatenate([jnp.zeros((k, x.shape[1]), x.dtype), x[:x.shape[0] - k]], axis=0)


def _linear_scan(p, u):
    t = p.shape[0]
    row = jax.lax.broadcasted_iota(jnp.int32, p.shape, 0)
    k = 1
    while k < t:
        if k < V7X_SUBLANES:
            pm = jnp.where(row < k, 0.0, p)
            u = u + pm * pltpu.roll(u, k, 0)
            p = pm * pltpu.roll(p, k, 0)
        else:
            u = u + p * _shift_rows(u, k)
            p = p * _shift_rows(p, k)
        k *= 2
    return u


def _mixer_kernel(x_ref, mod_ref, g_ref, w_in_ref, cw_ref, cb_ref, wax_ref, ba_ref, bx_ref,
                  lam_ref, lng_ref, lnb_ref, ws_ref, sb_ref, wba_ref, wbb_ref, wo_ref,
                  o_ref, xr_buf, h_carry):
    t = x_ref.shape[1]
    d = x_ref.shape[2]
    hd = d // RNN_HEADS
    gd = d // SGU_GROUPS
    pad = V7X_SUBLANES
    kw = cw_ref.shape[0]

    @pl.when(pl.program_id(1) == 0)
    def _():
        xr_buf[0:pad, :] = jnp.zeros((pad, d), jnp.float32)
        h_carry[...] = jnp.zeros_like(h_carry)

    x = x_ref[0]
    mod = mod_ref[0]
    shift1, scale1, gate1 = mod[:, 0:d], mod[:, d:2 * d], mod[:, 2 * d:3 * d]
    h = _mod_norm(x, g_ref[...] * (1.0 + scale1), shift1)
    hb = h.astype(jnp.bfloat16)

    xr_buf[pad:pad + t, :] = _bdot(hb, w_in_ref[:, 0:d])
    xc = cb_ref[...] + xr_buf[pad - kw + 1:pad - kw + 1 + t, :] * cw_ref[0:1, :]
    for k in range(1, kw):
        xc = xc + xr_buf[pad - kw + 1 + k:pad - kw + 1 + k + t, :] * cw_ref[k:k + 1, :]
    xr_buf[0:pad, :] = xr_buf[t:t + pad, :]

    xcb = xc.astype(jnp.bfloat16)
    pre = [_bdot(xcb[:, i * hd:(i + 1) * hd], wax_ref[i]) for i in range(RNN_HEADS)]
    r = _sigmoid(jnp.concatenate([p[:, :hd] for p in pre], axis=1) + ba_ref[...])
    gi = _sigmoid(jnp.concatenate([p[:, hd:] for p in pre], axis=1) + bx_ref[...])
    lam = lam_ref[...]
    log_sig = jnp.minimum(lam, 0.0) - jnp.log1p(jnp.exp(-jnp.abs(lam)))
    log_a = r * (LRU_C * log_sig)
    a = jnp.exp(log_a)
    mult = jnp.sqrt(-jnp.tanh(log_a) * (a * a + 1.0))
    u = mult * (gi * xc)
    row = jax.lax.broadcasted_iota(jnp.int32, u.shape, 0)
    u = u + jnp.where(row == 0, a * h_carry[...], 0.0)
    hs = _linear_scan(a, u)
    h_carry[...] = hs[t - 1:t, :]
    ya_pre = hs * _gelu(_bdot(hb, w_in_ref[:, d:2 * d]))
    ya = _bdot(ya_pre.astype(jnp.bfloat16), wba_ref[...])

    gu = _gelu(_bdot(hb, w_in_ref[:, 2 * d:3 * d]))
    gv = _gelu(_bdot(hb, w_in_ref[:, 3 * d:4 * d]))
    mu = jnp.mean(gv, axis=-1, keepdims=True)
    gc = gv - mu
    var = jnp.mean(gc * gc, axis=-1, keepdims=True)
    vb = ((gc * jax.lax.rsqrt(var + EPS)) * lng_ref[...] + lnb_ref[...]).astype(jnp.bfloat16)
    ri = jax.lax.broadcasted_iota(jnp.int32, (SGU_BLOCK, SGU_BLOCK), 0)
    ci = jax.lax.broadcasted_iota(jnp.int32, (SGU_BLOCK, SGU_BLOCK), 1)
    wm = [jnp.where(ci <= ri, ws_ref[g], 0.0).astype(jnp.bfloat16) for g in range(SGU_GROUPS)]
    blocks = []
    for n in range(t // SGU_BLOCK):
        rows = slice(n * SGU_BLOCK, (n + 1) * SGU_BLOCK)
        mixed = jnp.concatenate(
            [_bdot(wm[g], vb[rows, g * gd:(g + 1) * gd]) for g in range(SGU_GROUPS)], axis=1)
        blocks.append(mixed + sb_ref[...])
    yb_pre = gu * jnp.concatenate(blocks, axis=0)
    yb = _bdot(yb_pre.astype(jnp.bfloat16), wbb_ref[...])

    sa = _sigmoid(_bdot(hb, w_in_ref[:, 4 * d:5 * d]))
    sg = _sigmoid(_bdot(hb, w_in_ref[:, 5 * d:6 * d]))
    merged = sa * ya + sg * yb
    o_ref[0] = x + gate1 * _bdot(merged.astype(jnp.bfloat16), wo_ref[...])


def _const_spec(shape):
    zeros = (0,) * len(shape)
    return pl.BlockSpec(shape, lambda b, i: zeros, pipeline_mode=pl.Buffered(1))


def _mixer(x, mod3, norm_g, w_in, conv_w, conv_b, wax, b_a, b_x, lam, ln_g, ln_b, w_s, sgu_bias,
           w_ba, w_bb, w_out):
    bsz, seq, d = x.shape
    t = TIME_TILE
    consts = [norm_g, w_in, conv_w, conv_b, wax, b_a, b_x, lam, ln_g, ln_b, w_s, sgu_bias,
              w_ba, w_bb, w_out]
    return pl.pallas_call(
        _mixer_kernel,
        grid=(bsz, seq // t),
        in_specs=[
            pl.BlockSpec((1, t, d), lambda b, i: (b, i, 0)),
            pl.BlockSpec((1, 1, N_MOD * d), lambda b, i: (b, 0, 0)),
        ] + [_const_spec(a.shape) for a in consts],
        out_specs=pl.BlockSpec((1, t, d), lambda b, i: (b, i, 0)),
        out_shape=jax.ShapeDtypeStruct(x.shape, jnp.float32),
        scratch_shapes=[
            pltpu.VMEM((t + V7X_SUBLANES, d), jnp.float32),
            pltpu.VMEM((1, d), jnp.float32),
        ],
        compiler_params=pltpu.CompilerParams(
            dimension_semantics=("arbitrary", "arbitrary"), vmem_limit_bytes=VMEM_LIMIT_BYTES),
        name="mixer",
    )(x, mod3, *consts)


def _ffn_kernel(x_ref, mod_ref, g_ref, wup_ref, cw_ref, cb_ref, wdn_ref, gf_ref,
                o_ref, up_buf, *, final_norm):
    t = x_ref.shape[1]
    d = x_ref.shape[2]
    dff = wdn_ref.shape[0]
    pad = V7X_SUBLANES
    kw = cw_ref.shape[0]

    @pl.when(pl.program_id(1) == 0)
    def _():
        up_buf[0:pad, :] = jnp.zeros((pad, up_buf.shape[1]), jnp.float32)

    x = x_ref[0]
    mod = mod_ref[0]
    shift2, scale2, gate2 = mod[:, 3 * d:4 * d], mod[:, 4 * d:5 * d], mod[:, 5 * d:6 * d]
    hb = _mod_norm(x, g_ref[...] * (1.0 + scale2), shift2).astype(jnp.bfloat16)

    def conv_cols(c0):
        cols = slice(c0, c0 + d)
        up_buf[pad:pad + t, cols] = _bdot(hb, wup_ref[:, cols])
        y = cb_ref[:, cols] + up_buf[pad - kw + 1:pad - kw + 1 + t, cols] * cw_ref[0:1, cols]
        for k in range(1, kw):
            y = y + up_buf[pad - kw + 1 + k:pad - kw + 1 + k + t, cols] * cw_ref[k:k + 1, cols]
        up_buf[0:pad, cols] = up_buf[t:t + pad, cols]
        return y

    acc = None
    for j in range(dff // d):
        act = conv_cols(j * d)
        val = conv_cols(dff + j * d)
        part = _bdot((_gelu(act) * val).astype(jnp.bfloat16), wdn_ref[j * d:(j + 1) * d, :])
        acc = part if acc is None else acc + part
    y = x + gate2 * acc
    if final_norm:
        y = (y * jax.lax.rsqrt(jnp.mean(y * y, axis=-1, keepdims=True) + EPS)) * gf_ref[...]
    o_ref[0] = y


def _ffn(x, mod3, norm_g, w_up, conv_w, conv_b, w_down, norm_final_g, final_norm):
    bsz, seq, d = x.shape
    t = TIME_TILE
    consts = [norm_g, w_up, conv_w, conv_b, w_down, norm_final_g]
    return pl.pallas_call(
        functools.partial(_ffn_kernel, final_norm=final_norm),
        grid=(bsz, seq // t),
        in_specs=[
            pl.BlockSpec((1, t, d), lambda b, i: (b, i, 0)),
            pl.BlockSpec((1, 1, N_MOD * d), lambda b, i: (b, 0, 0)),
        ] + [_const_spec(a.shape) for a in consts],
        out_specs=pl.BlockSpec((1, t, d), lambda b, i: (b, i, 0)),
        out_shape=jax.ShapeDtypeStruct(x.shape, jnp.float32),
        scratch_shapes=[pltpu.VMEM((t + V7X_SUBLANES, w_up.shape[1]), jnp.float32)],
        compiler_params=pltpu.CompilerParams(
            dimension_semantics=("arbitrary", "arbitrary"), vmem_limit_bytes=VMEM_LIMIT_BYTES),
        name="ffn",
    )(x, mod3, *consts)


def kernel(x, c, w_ada, b_ada, norm_mix_g, w_in, rnn_conv_w, rnn_conv_b, lru_w_a, lru_b_a, lru_w_x, lru_b_x, lru_lambda, sgu_ln_g, sgu_ln_b, sgu_w_s, sgu_b_s, w_branch_a, w_branch_b, w_out, norm_ffn_g, w_up, ffn_conv_w, ffn_conv_b, w_down, norm_final_g):
    depth = w_in.shape[0]
    d = x.shape[-1]
    bf = jnp.bfloat16
    row = lambda v: v.reshape(1, -1)
    for l in range(depth):
        mod3 = _adaln(c, w_ada[l], b_ada[l]).reshape(x.shape[0], 1, N_MOD * d)
        wax = jnp.concatenate([lru_w_a[l], lru_w_x[l]], axis=-1).astype(bf)
        sgu_bias = jnp.repeat(sgu_b_s[l].T, d // SGU_GROUPS, axis=1)
        x = _mixer(x, mod3, row(norm_mix_g[l]), w_in[l].astype(bf), rnn_conv_w[l],
                   row(rnn_conv_b[l]), wax, row(lru_b_a[l]), row(lru_b_x[l]), row(lru_lambda[l]),
                   row(sgu_ln_g[l]), row(sgu_ln_b[l]), sgu_w_s[l], sgu_bias,
                   w_branch_a[l].astype(bf), w_branch_b[l].astype(bf), w_out[l].astype(bf))
        x = _ffn(x, mod3, row(norm_ffn_g[l]), w_up[l].astype(bf), ffn_conv_w[l],
                 row(ffn_conv_b[l]), w_down[l].astype(bf), row(norm_final_g),
                 final_norm=(l == depth - 1))
    return x
```

```python
import functools

import jax
import jax.numpy as jnp
from jax.experimental import pallas as pl
from jax.experimental.pallas import tpu as pltpu

EPS = 1e-6
LRU_C = 8.0
RNN_HEADS = 8
SGU_GROUPS = 8
SGU_BLOCK = 128
N_MOD = 6

V7X_SUBLANES = 8
TIME_TILE = 256
FFN_CHUNKS = 12
ADALN_COL_BLOCK = 1536
VMEM_LIMIT_BYTES = 56 * 1024 * 1024

_GELU_C0 = 0.7978845608028654
_GELU_C1 = 0.044715


def _gelu(x):
    inner = x * (_GELU_C0 + (_GELU_C0 * _GELU_C1) * (x * x))
    hx = 0.5 * x
    return hx + hx * jnp.tanh(inner)


def _sigmoid(x):
    return 0.5 * jnp.tanh(0.5 * x) + 0.5


def _bdot(a, b):
    return jnp.dot(a, b, preferred_element_type=jnp.float32)


def _pack_rows(w):
    wb = w.astype(jnp.bfloat16)
    *lead, k, n = wb.shape
    pairs = jnp.swapaxes(wb.reshape(*lead, k // 2, 2, n), -1, -2)
    return jax.lax.bitcast_convert_type(pairs, jnp.uint32)


def _unpack_rows(w):
    return pltpu.bitcast(w, jnp.bfloat16)


def _mod_norm(x, gs, shift):
    r = jax.lax.rsqrt(jnp.mean(x * x, axis=-1, keepdims=True) + EPS)
    return (x * r) * gs + shift


def _adaln_kernel(ct_ref, w_ref, b_ref, o_ref):
    ct = ct_ref[...]
    act = ct * _sigmoid(ct)
    w = w_ref[...]
    rows = []
    for b in range(ct.shape[1]):
        rows.append(jnp.sum(act[:, b:b + 1] * w, axis=0, keepdims=True))
    o_ref[...] = jnp.concatenate(rows, axis=0) + b_ref[...]


def _adaln(c, w_ada, b_ada):
    bsz, d = c.shape
    n = w_ada.shape[1]
    bn = ADALN_COL_BLOCK
    return pl.pallas_call(
        _adaln_kernel,
        grid=(n // bn,),
        in_specs=[
            pl.BlockSpec((d, bsz), lambda j: (0, 0)),
            pl.BlockSpec((d, bn), lambda j: (0, j)),
            pl.BlockSpec((1, bn), lambda j: (0, j)),
        ],
        out_specs=pl.BlockSpec((bsz, bn), lambda j: (0, j)),
        out_shape=jax.ShapeDtypeStruct((bsz, n), jnp.float32),
        compiler_params=pltpu.CompilerParams(
            dimension_semantics=("arbitrary",), vmem_limit_bytes=VMEM_LIMIT_BYTES),
        name="adaln",
    )(c.T, w_ada, b_ada.reshape(1, n))


def _shift_rows(x, k):
    return jnp.concatenate([jnp.zeros((k, x.shape[1]), x.dtype), x[:x.shape[0] - k]], axis=0)


def _linear_scan(p, u):
    t = p.shape[0]
    row = jax.lax.broadcasted_iota(jnp.int32, p.shape, 0)
    k = 1
    while k < t:
        if k < V7X_SUBLANES:
            pm = jnp.where(row < k, 0.0, p)
            u = u + pm * pltpu.roll(u, k, 0)
            p = pm * pltpu.roll(p, k, 0)
        else:
            u = u + p * _shift_rows(u, k)
            p = p * _shift_rows(p, k)
        k *= 2
    return u


def _mixer_kernel(x_ref, mod_ref, g_ref, w_in_ref, cw_ref, cb_ref, wax_ref, ba_ref, bx_ref,
                  lam_ref, lng_ref, lnb_ref, ws_ref, sb_ref, wba_ref, wbb_ref, wo_ref,
                  o_ref, xr_buf, h_carry):
    t = x_ref.shape[1]
    d = x_ref.shape[2]
    hd = d // RNN_HEADS
    gd = d // SGU_GROUPS
    pad = V7X_SUBLANES
    kw = cw_ref.shape[0]

    @pl.when(pl.program_id(1) == 0)
    def _():
        xr_buf[0:pad, :] = jnp.zeros((pad, d), jnp.float32)
        h_carry[...] = jnp.zeros_like(h_carry)

    x = x_ref[0]
    mod = mod_ref[0]
    shift1, scale1, gate1 = mod[:, 0:d], mod[:, d:2 * d], mod[:, 2 * d:3 * d]
    h = _mod_norm(x, g_ref[...] * (1.0 + scale1), shift1)
    hb = h.astype(jnp.bfloat16)

    def w_in_cols(j):
        return _unpack_rows(w_in_ref[:, j * d:(j + 1) * d])

    xr_buf[pad:pad + t, :] = _bdot(hb, w_in_cols(0))
    xc = cb_ref[...] + xr_buf[pad - kw + 1:pad - kw + 1 + t, :] * cw_ref[0:1, :]
    for k in range(1, kw):
        xc = xc + xr_buf[pad - kw + 1 + k:pad - kw + 1 + k + t, :] * cw_ref[k:k + 1, :]
    xr_buf[0:pad, :] = xr_buf[t:t + pad, :]

    xcb = xc.astype(jnp.bfloat16)
    pre = [_bdot(xcb[:, i * hd:(i + 1) * hd], _unpack_rows(wax_ref[i])) for i in range(RNN_HEADS)]
    r = _sigmoid(jnp.concatenate([p[:, :hd] for p in pre], axis=1) + ba_ref[...])
    gi = _sigmoid(jnp.concatenate([p[:, hd:] for p in pre], axis=1) + bx_ref[...])
    lam = lam_ref[...]
    log_sig = jnp.minimum(lam, 0.0) - jnp.log1p(jnp.exp(-jnp.abs(lam)))
    log_a = r * (LRU_C * log_sig)
    a = jnp.exp(log_a)
    mult = jnp.sqrt(-jnp.tanh(log_a) * (a * a + 1.0))
    u = mult * (gi * xc)
    row = jax.lax.broadcasted_iota(jnp.int32, u.shape, 0)
    u = u + jnp.where(row == 0, a * h_carry[...], 0.0)
    hs = _linear_scan(a, u)
    h_carry[...] = hs[t - 1:t, :]
    ya_pre = hs * _gelu(_bdot(hb, w_in_cols(1)))
    ya = _bdot(ya_pre.astype(jnp.bfloat16), _unpack_rows(wba_ref[...]))

    gu = _gelu(_bdot(hb, w_in_cols(2)))
    gv = _gelu(_bdot(hb, w_in_cols(3)))
    mu = jnp.mean(gv, axis=-1, keepdims=True)
    gc = gv - mu
    var = jnp.mean(gc * gc, axis=-1, keepdims=True)
    vb = ((gc * jax.lax.rsqrt(var + EPS)) * lng_ref[...] + lnb_ref[...]).astype(jnp.bfloat16)
    ri = jax.lax.broadcasted_iota(jnp.int32, (SGU_BLOCK, SGU_BLOCK), 0)
    ci = jax.lax.broadcasted_iota(jnp.int32, (SGU_BLOCK, SGU_BLOCK), 1)
    wm = [jnp.where(ci <= ri, ws_ref[g], 0.0).astype(jnp.bfloat16) for g in range(SGU_GROUPS)]
    blocks = []
    for n in range(t // SGU_BLOCK):
        rows = slice(n * SGU_BLOCK, (n + 1) * SGU_BLOCK)
        mixed = jnp.concatenate(
            [_bdot(wm[g], vb[rows, g * gd:(g + 1) * gd]) for g in range(SGU_GROUPS)], axis=1)
        blocks.append(mixed + sb_ref[...])
    yb_pre = gu * jnp.concatenate(blocks, axis=0)
    yb = _bdot(yb_pre.astype(jnp.bfloat16), _unpack_rows(wbb_ref[...]))

    sa = _sigmoid(_bdot(hb, w_in_cols(4)))
    sg = _sigmoid(_bdot(hb, w_in_cols(5)))
    merged = sa * ya + sg * yb
    o_ref[0] = x + gate1 * _bdot(merged.astype(jnp.bfloat16), _unpack_rows(wo_ref[...]))


def _const_spec(shape):
    zeros = (0,) * len(shape)
    return pl.BlockSpec(shape, lambda b, i: zeros, pipeline_mode=pl.Buffered(1))


def _mixer(x, mod3, norm_g, w_in, conv_w, conv_b, wax, b_a, b_x, lam, ln_g, ln_b, w_s, sgu_bias,
           w_ba, w_bb, w_out):
    bsz, seq, d = x.shape
    t = TIME_TILE
    consts = [norm_g, w_in, conv_w, conv_b, wax, b_a, b_x, lam, ln_g, ln_b, w_s, sgu_bias,
              w_ba, w_bb, w_out]
    return pl.pallas_call(
        _mixer_kernel,
        grid=(bsz, seq // t),
        in_specs=[
            pl.BlockSpec((1, t, d), lambda b, i: (b, i, 0)),
            pl.BlockSpec((1, 1, N_MOD * d), lambda b, i: (b, 0, 0)),
        ] + [_const_spec(a.shape) for a in consts],
        out_specs=pl.BlockSpec((1, t, d), lambda b, i: (b, i, 0)),
        out_shape=jax.ShapeDtypeStruct(x.shape, jnp.float32),
        scratch_shapes=[
            pltpu.VMEM((t + V7X_SUBLANES, d), jnp.float32),
            pltpu.VMEM((1, d), jnp.float32),
        ],
        compiler_params=pltpu.CompilerParams(
            dimension_semantics=("arbitrary", "arbitrary"), vmem_limit_bytes=VMEM_LIMIT_BYTES),
        name="mixer",
    )(x, mod3, *consts)


def _ffn_kernel(xin_ref, xres_ref, modin_ref, modres_ref, g_ref, wup_ref, cw_ref, cb_ref,
                wdn_ref, gf_ref, o_ref, up_buf, carry_buf, hb_buf, acc_buf, p_buf,
                *, final_norm, tiles_per_seq):
    t = xin_ref.shape[1]
    d = xin_ref.shape[2]
    n_chunks = wup_ref.shape[0]
    cw = wup_ref.shape[2] // 2
    pad = V7X_SUBLANES
    kw = cw_ref.shape[1]
    step = pl.program_id(0)
    seq_start = jax.lax.rem(step - 1, tiles_per_seq) == 0

    @pl.when(step == 0)
    def _():
        up_buf[...] = jnp.zeros_like(up_buf)
        carry_buf[...] = jnp.zeros_like(carry_buf)

    def run(slot_w):
        slot_r = 1 - slot_w
        mod = modin_ref[0]
        shift2, scale2 = mod[:, 3 * d:4 * d], mod[:, 4 * d:5 * d]
        hb_buf[...] = _mod_norm(
            xin_ref[0], g_ref[...] * (1.0 + scale2), shift2).astype(jnp.bfloat16)

        def down_acc(k):
            part = _bdot(p_buf[k % 2], _unpack_rows(wdn_ref[k]))
            if k == 0:
                acc_buf[...] = part
            else:
                acc_buf[...] += part

        for k in range(n_chunks):
            up_buf[slot_w, k, pad:pad + t, :] = _bdot(hb_buf[...], _unpack_rows(wup_ref[k]))
            up_buf[slot_r, k, 0:pad, :] = jnp.where(seq_start, 0.0, carry_buf[k])
            base = pad - kw + 1
            y = cb_ref[k] + up_buf[slot_r, k, base:base + t, :] * cw_ref[k, 0:1, :]
            for j in range(1, kw):
                y = y + up_buf[slot_r, k, base + j:base + j + t, :] * cw_ref[k, j:j + 1, :]
            carry_buf[k] = up_buf[slot_r, k, t:t + pad, :]
            if k > 0:
                down_acc(k - 1)
            p_buf[k % 2] = (_gelu(y[:, :cw]) * y[:, cw:]).astype(jnp.bfloat16)
        down_acc(n_chunks - 1)

        gate2 = modres_ref[0][:, 5 * d:6 * d]
        y = xres_ref[0] + gate2 * acc_buf[...]
        if final_norm:
            y = (y * jax.lax.rsqrt(jnp.mean(y * y, axis=-1, keepdims=True) + EPS)) * gf_ref[...]
        o_ref[0] = y

    parity = jax.lax.rem(step, 2)
    pl.when(parity == 0)(functools.partial(run, 0))
    pl.when(parity == 1)(functools.partial(run, 1))


def _ffn(x, mod3, norm_g, w_up, conv_w, conv_b, w_down, norm_final_g, final_norm):
    bsz, seq, d = x.shape
    t = TIME_TILE
    dff = w_down.shape[0]
    nc = FFN_CHUNKS
    cw = dff // nc
    n_t = seq // t
    n_tiles = bsz * n_t
    wup3 = _pack_rows(w_up.reshape(d, 2, nc, cw).transpose(2, 0, 1, 3).reshape(nc, d, 2 * cw))
    cw3 = conv_w.reshape(-1, 2, nc, cw).transpose(2, 0, 1, 3).reshape(nc, -1, 2 * cw)
    cb3 = conv_b.reshape(2, nc, cw).transpose(1, 0, 2).reshape(nc, 1, 2 * cw)
    wdn3 = _pack_rows(w_down.reshape(nc, cw, d))
    consts = [norm_g, wup3, cw3, cb3, wdn3, norm_final_g]

    def const_spec(shape):
        zeros = (0,) * len(shape)
        return pl.BlockSpec(shape, lambda g: zeros, pipeline_mode=pl.Buffered(1))

    tile_in = lambda g: jnp.minimum(g, n_tiles - 1)
    tile_res = lambda g: jnp.maximum(g - 1, 0)
    out = pl.pallas_call(
        functools.partial(_ffn_kernel, final_norm=final_norm, tiles_per_seq=n_t),
        grid=(n_tiles + 1,),
        in_specs=[
            pl.BlockSpec((1, t, d), lambda g: (tile_in(g), 0, 0)),
            pl.BlockSpec((1, t, d), lambda g: (tile_res(g), 0, 0)),
            pl.BlockSpec((1, 1, N_MOD * d), lambda g: (tile_in(g) // n_t, 0, 0)),
            pl.BlockSpec((1, 1, N_MOD * d), lambda g: (tile_res(g) // n_t, 0, 0)),
        ] + [const_spec(a.shape) for a in consts],
        out_specs=pl.BlockSpec((1, t, d), lambda g: (tile_res(g), 0, 0)),
        out_shape=jax.ShapeDtypeStruct((n_tiles, t, d), jnp.float32),
        scratch_shapes=[
            pltpu.VMEM((2, nc, t + V7X_SUBLANES, 2 * cw), jnp.float32),
            pltpu.VMEM((nc, V7X_SUBLANES, 2 * cw), jnp.float32),
            pltpu.VMEM((t, d), jnp.bfloat16),
            pltpu.VMEM((t, d), jnp.float32),
            pltpu.VMEM((2, t, cw), jnp.bfloat16),
        ],
        compiler_params=pltpu.CompilerParams(
            dimension_semantics=("arbitrary",), vmem_limit_bytes=VMEM_LIMIT_BYTES),
        name="ffn",
    )(x.reshape(n_tiles, t, d), x.reshape(n_tiles, t, d), mod3, mod3, *consts)
    return out.reshape(bsz, seq, d)


def kernel(x, c, w_ada, b_ada, norm_mix_g, w_in, rnn_conv_w, rnn_conv_b, lru_w_a, lru_b_a, lru_w_x, lru_b_x, lru_lambda, sgu_ln_g, sgu_ln_b, sgu_w_s, sgu_b_s, w_branch_a, w_branch_b, w_out, norm_ffn_g, w_up, ffn_conv_w, ffn_conv_b, w_down, norm_final_g):
    depth = w_in.shape[0]
    d = x.shape[-1]
    row = lambda v: v.reshape(1, -1)
    for l in range(depth):
        mod3 = _adaln(c, w_ada[l], b_ada[l]).reshape(x.shape[0], 1, N_MOD * d)
        wax = _pack_rows(jnp.concatenate([lru_w_a[l], lru_w_x[l]], axis=-1))
        sgu_bias = jnp.repeat(sgu_b_s[l].T, d // SGU_GROUPS, axis=1)
        x = _mixer(x, mod3, row(norm_mix_g[l]), _pack_rows(w_in[l]), rnn_conv_w[l],
                   row(rnn_conv_b[l]), wax, row(lru_b_a[l]), row(lru_b_x[l]), row(lru_lambda[l]),
                   row(sgu_ln_g[l]), row(sgu_ln_b[l]), sgu_w_s[l], sgu_bias,
                   _pack_rows(w_branch_a[l]), _pack_rows(w_branch_b[l]), _pack_rows(w_out[l]))
        x = _ffn(x, mod3, row(norm_ffn_g[l]), w_up[l], ffn_conv_w[l],
                 ffn_conv_b[l], w_down[l], row(norm_final_g),
                 final_norm=(l == depth - 1))
    return x
```

```python
import functools

import jax
import jax.numpy as jnp
from jax.experimental import pallas as pl
from jax.experimental.pallas import tpu as pltpu

EPS = 1e-6
LRU_C = 8.0
RNN_HEADS = 8
SGU_GROUPS = 8
SGU_BLOCK = 128
N_MOD = 6

V7X_SUBLANES = 8
V7X_LANES = 128
TIME_TILE = 256
FFN_CHUNKS = 12
FFN_ROW_PIECE = 64
FFN_DOWN_GROUP = 3
FFN_DOWN_AFTER = (4, 7, 11, 11)
ADALN_COL_BLOCK = 1536
PACK_COL_BLOCK = 512
VMEM_LIMIT_BYTES = 56 * 1024 * 1024

_GELU_C0 = 0.7978845608028654
_GELU_C1 = 0.044715


def _gelu(x):
    inner = x * (_GELU_C0 + (_GELU_C0 * _GELU_C1) * (x * x))
    hx = 0.5 * x
    return hx + hx * jnp.tanh(inner)


def _sigmoid(x):
    return 0.5 * jnp.tanh(0.5 * x) + 0.5


def _bdot(a, b):
    return jnp.dot(a, b, preferred_element_type=jnp.float32)


def _pack_kernel(w_ref, o_ref):
    o_ref[...] = pltpu.bitcast(w_ref[...].astype(jnp.bfloat16), jnp.uint32)


def _pack_rows(w, block_cols=PACK_COL_BLOCK, src_block=None):
    k, n = w.shape
    block_cols = min(block_cols, n)
    src = src_block or (lambda j: j)
    return pl.pallas_call(
        _pack_kernel,
        grid=(n // block_cols,),
        in_specs=[pl.BlockSpec((k, block_cols), lambda j: (0, src(j)))],
        out_specs=pl.BlockSpec((k // 2, block_cols), lambda j: (0, j)),
        out_shape=jax.ShapeDtypeStruct((k // 2, n), jnp.uint32),
        compiler_params=pltpu.CompilerParams(
            dimension_semantics=("arbitrary",), vmem_limit_bytes=VMEM_LIMIT_BYTES),
        name="pack_bf16",
    )(w)


def _unpack_rows(w):
    return pltpu.bitcast(w, jnp.bfloat16)


def _mod_norm(x, gs, shift):
    r = jax.lax.rsqrt(jnp.mean(x * x, axis=-1, keepdims=True) + EPS)
    return (x * r) * gs + shift


def _adaln_kernel(ct_ref, w_ref, b_ref, o_ref):
    ct = ct_ref[...]
    act = ct * _sigmoid(ct)
    w = w_ref[...]
    rows = []
    for b in range(ct.shape[1]):
        rows.append(jnp.sum(act[:, b:b + 1] * w, axis=0, keepdims=True))
    o_ref[...] = jnp.concatenate(rows, axis=0) + b_ref[...]


def _adaln(c, w_ada, b_ada):
    bsz, d = c.shape
    n = w_ada.shape[1]
    bn = ADALN_COL_BLOCK
    return pl.pallas_call(
        _adaln_kernel,
        grid=(n // bn,),
        in_specs=[
            pl.BlockSpec((d, bsz), lambda j: (0, 0)),
            pl.BlockSpec((d, bn), lambda j: (0, j)),
            pl.BlockSpec((1, bn), lambda j: (0, j)),
        ],
        out_specs=pl.BlockSpec((bsz, bn), lambda j: (0, j)),
        out_shape=jax.ShapeDtypeStruct((bsz, n), jnp.float32),
        compiler_params=pltpu.CompilerParams(
            dimension_semantics=("arbitrary",), vmem_limit_bytes=VMEM_LIMIT_BYTES),
        name="adaln",
    )(c.T, w_ada, b_ada.reshape(1, n))


def _shift_rows(x, k):
    return jnp.concatenate([jnp.zeros((k, x.shape[1]), x.dtype), x[:x.shape[0] - k]], axis=0)


def _linear_scan(p, u):
    t = p.shape[0]
    row = jax.lax.broadcasted_iota(jnp.int32, p.shape, 0)
    k = 1
    while k < t:
        if k < V7X_SUBLANES:
            pm = jnp.where(row < k, 0.0, p)
            u = u + pm * pltpu.roll(u, k, 0)
            p = pm * pltpu.roll(p, k, 0)
        else:
            u = u + p * _shift_rows(u, k)
            p = p * _shift_rows(p, k)
        k *= 2
    return u


def _mixer_kernel(x_ref, mod_ref, g_ref, w_in_ref, cw_ref, cb_ref, wax_ref, ba_ref, bx_ref,
                  lam_ref, lng_ref, lnb_ref, ws_ref, sb_ref, wba_ref, wbb_ref, wo_ref,
                  o_ref, xr_buf, h_carry):
    t = x_ref.shape[1]
    d = x_ref.shape[2]
    hd = d // RNN_HEADS
    gd = d // SGU_GROUPS
    pad = V7X_SUBLANES
    kw = cw_ref.shape[0]

    @pl.when(pl.program_id(1) == 0)
    def _():
        xr_buf[0:pad, :] = jnp.zeros((pad, d), jnp.float32)
        h_carry[...] = jnp.zeros_like(h_carry)

    x = x_ref[0]
    mod = mod_ref[0]
    shift1, scale1, gate1 = mod[:, 0:d], mod[:, d:2 * d], mod[:, 2 * d:3 * d]
    h = _mod_norm(x, g_ref[...] * (1.0 + scale1), shift1)
    hb = h.astype(jnp.bfloat16)

    def w_in_cols(j):
        return _unpack_rows(w_in_ref[:, j * d:(j + 1) * d])

    xr_buf[pad:pad + t, :] = _bdot(hb, w_in_cols(0))
    xc = cb_ref[...] + xr_buf[pad - kw + 1:pad - kw + 1 + t, :] * cw_ref[0:1, :]
    for k in range(1, kw):
        xc = xc + xr_buf[pad - kw + 1 + k:pad - kw + 1 + k + t, :] * cw_ref[k:k + 1, :]
    xr_buf[0:pad, :] = xr_buf[t:t + pad, :]

    xcb = xc.astype(jnp.bfloat16)
    pre = [_bdot(xcb[:, i * hd:(i + 1) * hd], _unpack_rows(wax_ref[i])) for i in range(RNN_HEADS)]
    r = _sigmoid(jnp.concatenate([p[:, :hd] for p in pre], axis=1) + ba_ref[...])
    gi = _sigmoid(jnp.concatenate([p[:, hd:] for p in pre], axis=1) + bx_ref[...])
    lam = lam_ref[...]
    log_sig = jnp.minimum(lam, 0.0) - jnp.log1p(jnp.exp(-jnp.abs(lam)))
    log_a = r * (LRU_C * log_sig)
    a = jnp.exp(log_a)
    mult = jnp.sqrt(-jnp.tanh(log_a) * (a * a + 1.0))
    u = mult * (gi * xc)
    row = jax.lax.broadcasted_iota(jnp.int32, u.shape, 0)
    u = u + jnp.where(row == 0, a * h_carry[...], 0.0)
    hs = _linear_scan(a, u)
    h_carry[...] = hs[t - 1:t, :]
    ya_pre = hs * _gelu(_bdot(hb, w_in_cols(1)))
    ya = _bdot(ya_pre.astype(jnp.bfloat16), _unpack_rows(wba_ref[...]))

    gu = _gelu(_bdot(hb, w_in_cols(2)))
    gv = _gelu(_bdot(hb, w_in_cols(3)))
    mu = jnp.mean(gv, axis=-1, keepdims=True)
    gc = gv - mu
    var = jnp.mean(gc * gc, axis=-1, keepdims=True)
    vb = ((gc * jax.lax.rsqrt(var + EPS)) * lng_ref[...] + lnb_ref[...]).astype(jnp.bfloat16)
    ri = jax.lax.broadcasted_iota(jnp.int32, (SGU_BLOCK, SGU_BLOCK), 0)
    ci = jax.lax.broadcasted_iota(jnp.int32, (SGU_BLOCK, SGU_BLOCK), 1)
    wm = [jnp.where(ci <= ri, ws_ref[g], 0.0).astype(jnp.bfloat16) for g in range(SGU_GROUPS)]
    blocks = []
    for n in range(t // SGU_BLOCK):
        rows = slice(n * SGU_BLOCK, (n + 1) * SGU_BLOCK)
        mixed = jnp.concatenate(
            [_bdot(wm[g], vb[rows, g * gd:(g + 1) * gd]) for g in range(SGU_GROUPS)], axis=1)
        blocks.append(mixed + sb_ref[...])
    yb_pre = gu * jnp.concatenate(blocks, axis=0)
    yb = _bdot(yb_pre.astype(jnp.bfloat16), _unpack_rows(wbb_ref[...]))

    sa = _sigmoid(_bdot(hb, w_in_cols(4)))
    sg = _sigmoid(_bdot(hb, w_in_cols(5)))
    merged = sa * ya + sg * yb
    o_ref[0] = x + gate1 * _bdot(merged.astype(jnp.bfloat16), _unpack_rows(wo_ref[...]))


def _const_spec(shape):
    zeros = (0,) * len(shape)
    return pl.BlockSpec(shape, lambda b, i: zeros, pipeline_mode=pl.Buffered(1))


def _mixer(x, mod3, norm_g, w_in, conv_w, conv_b, wax, b_a, b_x, lam, ln_g, ln_b, w_s, sgu_bias,
           w_ba, w_bb, w_out):
    bsz, seq, d = x.shape
    t = TIME_TILE
    consts = [norm_g, w_in, conv_w, conv_b, wax, b_a, b_x, lam, ln_g, ln_b, w_s, sgu_bias,
              w_ba, w_bb, w_out]
    return pl.pallas_call(
        _mixer_kernel,
        grid=(bsz, seq // t),
        in_specs=[
            pl.BlockSpec((1, t, d), lambda b, i: (b, i, 0)),
            pl.BlockSpec((1, 1, N_MOD * d), lambda b, i: (b, 0, 0)),
        ] + [_const_spec(a.shape) for a in consts],
        out_specs=pl.BlockSpec((1, t, d), lambda b, i: (b, i, 0)),
        out_shape=jax.ShapeDtypeStruct(x.shape, jnp.float32),
        scratch_shapes=[
            pltpu.VMEM((t + V7X_SUBLANES, d), jnp.float32),
            pltpu.VMEM((1, d), jnp.float32),
        ],
        compiler_params=pltpu.CompilerParams(
            dimension_semantics=("arbitrary", "arbitrary"), vmem_limit_bytes=VMEM_LIMIT_BYTES),
        name="mixer",
    )(x, mod3, *consts)


def _ffn_kernel(x0_ref, xnext_ref, xres_ref, mod_ref, g_ref, wup_ref, cw_ref, cb_ref,
                wdn_ref, gf_ref, o_ref, up_buf, hb_cur, hb_next, acc_buf, p_buf,
                *, final_norm, tiles_per_seq, n_tiles, n_chunks):
    t = xnext_ref.shape[1]
    d = xnext_ref.shape[2]
    cw = wdn_ref.shape[0] * 2 // n_chunks
    pad = V7X_SUBLANES
    kw = cw_ref.shape[0]
    step = pl.program_id(0)
    seq_start = jax.lax.rem(step, tiles_per_seq) == 0
    g_norm = g_ref[...]

    def norm_rows(x, b):
        mod = mod_ref[b]
        return _mod_norm(x, g_norm * (1.0 + mod[:, 4 * d:5 * d]),
                         mod[:, 3 * d:4 * d]).astype(jnp.bfloat16)

    @pl.when(step == 0)
    def _():
        hb_next[...] = norm_rows(x0_ref[0], 0)
        acc_buf[...] = jnp.zeros_like(acc_buf)
        up_buf[:, t:t + pad, :] = jnp.zeros((n_chunks, pad, 2 * cw), jnp.float32)

    hb_cur[...] = hb_next[...]

    b_res = jnp.maximum(step - 1, 0) // tiles_per_seq
    b_next = jnp.minimum(step + 1, n_tiles - 1) // tiles_per_seq
    gate2 = mod_ref[b_res][:, 5 * d:6 * d]
    for r in range(0, t, FFN_ROW_PIECE):
        rows = slice(r, r + FFN_ROW_PIECE)
        y = xres_ref[0, rows, :] + gate2 * acc_buf[rows, :]
        if final_norm:
            y = (y * jax.lax.rsqrt(jnp.mean(y * y, axis=-1, keepdims=True) + EPS)) * gf_ref[...]
        o_ref[0, rows, :] = y
        hb_next[rows, :] = norm_rows(xnext_ref[0, rows, :], b_next)

    def down_group(g):
        lo, hi = g * FFN_DOWN_GROUP * cw, (g + 1) * FFN_DOWN_GROUP * cw
        part = _bdot(p_buf[:, lo:hi], _unpack_rows(wdn_ref[lo // 2:hi // 2, :]))
        if g == 0:
            acc_buf[...] = part
        else:
            acc_buf[...] += part

    base = pad - kw + 1
    for k in range(n_chunks):
        cols = slice(k * 2 * cw, (k + 1) * 2 * cw)
        up_buf[k, 0:pad, :] = jnp.where(seq_start, 0.0, up_buf[k, t:t + pad, :])
        up_buf[k, pad:pad + t, :] = _bdot(hb_cur[...], _unpack_rows(wup_ref[:, cols]))
        y = cb_ref[:, cols] + up_buf[k, base:base + t, :] * cw_ref[0:1, cols]
        for j in range(1, kw):
            y = y + up_buf[k, base + j:base + j + t, :] * cw_ref[j:j + 1, cols]
        p_buf[:, k * cw:(k + 1) * cw] = (_gelu(y[:, :cw]) * y[:, cw:]).astype(jnp.bfloat16)
        for g, after in enumerate(FFN_DOWN_AFTER):
            if after == k:
                down_group(g)


def _ffn(x, mod3, norm_g, w_up, conv_w, conv_b, w_down, norm_final_g, final_norm):
    bsz, seq, d = x.shape
    t = TIME_TILE
    dff = w_down.shape[0]
    nc = FFN_CHUNKS
    cw = dff // nc
    n_t = seq // t
    n_tiles = bsz * n_t
    perm = lambda a: a.reshape(-1, 2, nc, cw).transpose(0, 2, 1, 3).reshape(-1, 2 * dff)
    wup = _pack_rows(w_up, block_cols=cw, src_block=lambda j: (j % 2) * nc + j // 2)
    consts = [mod3, norm_g, wup, perm(conv_w), perm(conv_b), _pack_rows(w_down), norm_final_g]

    def const_spec(shape):
        zeros = (0,) * len(shape)
        return pl.BlockSpec(shape, lambda g: zeros, pipeline_mode=pl.Buffered(1))

    xt = x.reshape(n_tiles, t, d)
    tile_res = lambda g: jnp.maximum(g - 1, 0)
    out = pl.pallas_call(
        functools.partial(_ffn_kernel, final_norm=final_norm, tiles_per_seq=n_t,
                          n_tiles=n_tiles, n_chunks=nc),
        grid=(n_tiles + 1,),
        in_specs=[
            pl.BlockSpec((1, t, d), lambda g: (0, 0, 0), pipeline_mode=pl.Buffered(1)),
            pl.BlockSpec((1, t, d), lambda g: (jnp.minimum(g + 1, n_tiles - 1), 0, 0)),
            pl.BlockSpec((1, t, d), lambda g: (tile_res(g), 0, 0)),
        ] + [const_spec(a.shape) for a in consts],
        out_specs=pl.BlockSpec((1, t, d), lambda g: (tile_res(g), 0, 0)),
        out_shape=jax.ShapeDtypeStruct((n_tiles, t, d), jnp.float32),
        scratch_shapes=[
            pltpu.VMEM((nc, t + V7X_SUBLANES, 2 * cw), jnp.float32),
            pltpu.VMEM((t, d), jnp.bfloat16),
            pltpu.VMEM((t, d), jnp.bfloat16),
            pltpu.VMEM((t, d), jnp.float32),
            pltpu.VMEM((t, dff), jnp.bfloat16),
        ],
        compiler_params=pltpu.CompilerParams(
            dimension_semantics=("arbitrary",), vmem_limit_bytes=VMEM_LIMIT_BYTES),
        name="ffn",
    )(xt, xt, xt, *consts)
    return out.reshape(bsz, seq, d)


def kernel(x, c, w_ada, b_ada, norm_mix_g, w_in, rnn_conv_w, rnn_conv_b, lru_w_a, lru_b_a, lru_w_x, lru_b_x, lru_lambda, sgu_ln_g, sgu_ln_b, sgu_w_s, sgu_b_s, w_branch_a, w_branch_b, w_out, norm_ffn_g, w_up, ffn_conv_w, ffn_conv_b, w_down, norm_final_g):
    depth = w_in.shape[0]
    d = x.shape[-1]
    row = lambda v: v.reshape(1, -1)
    for l in range(depth):
        mod3 = _adaln(c, w_ada[l], b_ada[l]).reshape(x.shape[0], 1, N_MOD * d)
        hd = d // RNN_HEADS
        wax = jnp.concatenate([lru_w_a[l], lru_w_x[l]], axis=-1)
        wax = _pack_rows(wax.reshape(d, 2 * hd)).reshape(RNN_HEADS, hd // 2, 2 * hd)
        sgu_bias = jnp.repeat(sgu_b_s[l].T, d // SGU_GROUPS, axis=1)
        x = _mixer(x, mod3, row(norm_mix_g[l]), _pack_rows(w_in[l]), rnn_conv_w[l],
                   row(rnn_conv_b[l]), wax, row(lru_b_a[l]), row(lru_b_x[l]), row(lru_lambda[l]),
                   row(sgu_ln_g[l]), row(sgu_ln_b[l]), sgu_w_s[l], sgu_bias,
                   _pack_rows(w_branch_a[l]), _pack_rows(w_branch_b[l]), _pack_rows(w_out[l]))
        x = _ffn(x, mod3, row(norm_ffn_g[l]), w_up[l], ffn_conv_w[l],
                 ffn_conv_b[l], w_down[l], row(norm_final_g),
                 final_norm=(l == depth - 1))
    return x
```

```python
import functools

import jax
import jax.numpy as jnp
import numpy as np
from jax.experimental import pallas as pl
from jax.experimental.pallas import tpu as pltpu

EPS = 1e-6
LRU_C = 8.0
RNN_HEADS = 8
SGU_GROUPS = 8
SGU_BLOCK = 128
N_MOD = 6

V7X_SUBLANES = 8
TIME_TILE = 256
SEG = SGU_BLOCK // V7X_SUBLANES
FFN_CHUNKS = 12
FFN_ROW_PIECE = 64
FFN_DOWN_GROUP = 3
FFN_DOWN_AFTER = (4, 7, 11, 11)
ADALN_COL_BLOCK = 1536
PACK_COL_BLOCK = 512
VMEM_LIMIT_BYTES = 56 * 1024 * 1024

_GELU_C0 = 0.7978845608028654
_GELU_C1 = 0.044715


def _gelu(x):
    inner = x * (_GELU_C0 + (_GELU_C0 * _GELU_C1) * (x * x))
    hx = 0.5 * x
    return hx + hx * jnp.tanh(inner)


def _sigmoid(x):
    return 0.5 * jnp.tanh(0.5 * x) + 0.5


def _bdot(a, b):
    return jnp.dot(a, b, preferred_element_type=jnp.float32)


def _unpack_rows(w):
    return pltpu.bitcast(w, jnp.bfloat16)


def _mod_norm(x, gs, shift):
    r = jax.lax.rsqrt(jnp.mean(x * x, axis=-1, keepdims=True) + EPS)
    return (x * r) * gs + shift


def _block_time_index():
    rho = np.arange(SGU_BLOCK)
    return (rho % V7X_SUBLANES) * SEG + rho // V7X_SUBLANES


def _sublane_iota(cols):
    return jax.lax.broadcasted_iota(jnp.int32, (V7X_SUBLANES, cols), 0)


def _boundary_groups(blk_tail, prev_tail):
    sub = _sublane_iota(blk_tail.shape[1])
    groups = []
    for i in range(0, blk_tail.shape[0], V7X_SUBLANES):
        cur = pltpu.roll(blk_tail[i:i + V7X_SUBLANES], 1, 0)
        prv = pltpu.roll(prev_tail[i:i + V7X_SUBLANES], 1, 0)
        groups.append(jnp.where(sub == 0, prv, cur))
    return jnp.concatenate(groups, axis=0)


def _causal_conv_block(x_blk, prev_tail, taps, bias):
    halo = prev_tail.shape[0]
    n = x_blk.shape[0]
    ext = jnp.concatenate([_boundary_groups(x_blk[n - halo:], prev_tail), x_blk], axis=0)
    y = bias + ext[0:n] * taps[0]
    for k in range(1, len(taps)):
        y = y + ext[k * V7X_SUBLANES:k * V7X_SUBLANES + n] * taps[k]
    return y


def _sublane_scan(a, b, carry):
    sub = _sublane_iota(a.shape[1])
    b = b + jnp.where(sub == 0, a * carry, 0.0)
    k = 1
    while k < V7X_SUBLANES:
        am = jnp.where(sub < k, 0.0, a)
        b = b + am * pltpu.roll(b, k, 0)
        a = am * pltpu.roll(a, k, 0)
        k *= 2
    return b


def _linear_scan_block(a, u, carry):
    n_groups = a.shape[0] // V7X_SUBLANES
    grp = lambda v, g: v[g * V7X_SUBLANES:(g + 1) * V7X_SUBLANES]
    hs, ps = [grp(u, 0)], [grp(a, 0)]
    for g in range(1, n_groups):
        hs.append(grp(a, g) * hs[-1] + grp(u, g))
        ps.append(grp(a, g) * ps[-1])
    seg_end = _sublane_scan(ps[-1], hs[-1], carry)
    sub = _sublane_iota(a.shape[1])
    seg_in = jnp.where(sub == 0, carry, pltpu.roll(seg_end, 1, 0))
    h = jnp.concatenate([hh + pp * seg_in for hh, pp in zip(hs, ps)], axis=0)
    return h, seg_end[V7X_SUBLANES - 1:V7X_SUBLANES, :]


def _tile_row_copies(hbm_rows, vmem_tile, first_row, sem, to_vmem):
    copies = []
    for n in range(vmem_tile.shape[0] // SEG):
        for s in range(V7X_SUBLANES):
            hbm = hbm_rows.at[pl.ds(first_row + n * SGU_BLOCK + s * SEG, SEG)]
            vmem = vmem_tile.at[pl.ds(n * SEG, SEG), pl.ds(s, 1)]
            src, dst = (hbm, vmem) if to_vmem else (vmem, hbm)
            copies.append(pltpu.make_async_copy(src, dst, sem))
    return copies


def _pack_kernel(w_ref, o_ref):
    o_ref[...] = pltpu.bitcast(w_ref[...].astype(jnp.bfloat16), jnp.uint32)


def _pack_rows(w, block_cols=PACK_COL_BLOCK, src_block=None):
    k, n = w.shape
    block_cols = min(block_cols, n)
    src = src_block or (lambda j: j)
    return pl.pallas_call(
        _pack_kernel,
        grid=(n // block_cols,),
        in_specs=[pl.BlockSpec((k, block_cols), lambda j: (0, src(j)))],
        out_specs=pl.BlockSpec((k // 2, block_cols), lambda j: (0, j)),
        out_shape=jax.ShapeDtypeStruct((k // 2, n), jnp.uint32),
        compiler_params=pltpu.CompilerParams(
            dimension_semantics=("arbitrary",), vmem_limit_bytes=VMEM_LIMIT_BYTES),
        name="pack_bf16",
    )(w)


def _adaln_kernel(ct_ref, w_ref, b_ref, o_ref):
    ct = ct_ref[...]
    act = ct * _sigmoid(ct)
    w = w_ref[...]
    rows = []
    for b in range(ct.shape[1]):
        rows.append(jnp.sum(act[:, b:b + 1] * w, axis=0, keepdims=True))
    o_ref[...] = jnp.concatenate(rows, axis=0) + b_ref[...]


def _adaln(c, w_ada, b_ada):
    bsz, d = c.shape
    n = w_ada.shape[1]
    bn = ADALN_COL_BLOCK
    return pl.pallas_call(
        _adaln_kernel,
        grid=(n // bn,),
        in_specs=[
            pl.BlockSpec((d, bsz), lambda j: (0, 0)),
            pl.BlockSpec((d, bn), lambda j: (0, j)),
            pl.BlockSpec((1, bn), lambda j: (0, j)),
        ],
        out_specs=pl.BlockSpec((bsz, bn), lambda j: (0, j)),
        out_shape=jax.ShapeDtypeStruct((bsz, n), jnp.float32),
        compiler_params=pltpu.CompilerParams(
            dimension_semantics=("arbitrary",), vmem_limit_bytes=VMEM_LIMIT_BYTES),
        name="adaln",
    )(c.T, w_ada, b_ada.reshape(1, n))


def _const_spec(shape):
    zeros = (0,) * len(shape)
    return pl.BlockSpec(shape, lambda g: zeros, pipeline_mode=pl.Buffered(1))


def _mixer_kernel(x_hbm, mod_ref, g_ref, w_in_ref, cw_ref, cb_ref, wax_ref, ba_ref, bx_ref,
                  lam_ref, lng_ref, lnb_ref, ws_ref, sb_ref, wba_ref, wbb_ref, wo_ref,
                  o_ref, xbuf, sems, z_buf, hb_buf, xr_tail, h_carry, merged_buf,
                  *, tiles_per_seq, n_tiles):
    t, d = o_ref.shape[1], o_ref.shape[2]
    hd = d // RNN_HEADS
    gd = d // SGU_GROUPS
    kw = cw_ref.shape[0]
    halo = (kw - 1) * V7X_SUBLANES
    n_blocks = t // SGU_BLOCK
    step = pl.program_id(0)
    tile = jnp.minimum(step, n_tiles - 1)
    slot = jax.lax.rem(tile, 3)
    slot_prev = jax.lax.rem(step + 2, 3)
    seq_start = jax.lax.rem(tile, tiles_per_seq) == 0

    def x_copies(tile_idx, slot_idx):
        return _tile_row_copies(x_hbm, xbuf.at[slot_idx], tile_idx * t, sems.at[slot_idx], True)

    @pl.when(step == 0)
    def _():
        for cp in x_copies(0, 0):
            cp.start()
        merged_buf[...] = jnp.zeros_like(merged_buf)
        xbuf[2] = jnp.zeros(xbuf.shape[1:], jnp.float32)

    @pl.when(step < n_tiles)
    def _():
        for cp in x_copies(step, slot):
            cp.wait()

    @pl.when(step + 1 < n_tiles)
    def _():
        for cp in x_copies(step + 1, jax.lax.rem(step + 1, 3)):
            cp.start()

    gate1_prev = mod_ref[jnp.maximum(step - 1, 0) // tiles_per_seq][:, 2 * d:3 * d]
    mo = _bdot(merged_buf[...], _unpack_rows(wo_ref[...]))
    o_ref[0] = xbuf[slot_prev].reshape(t, d) + gate1_prev * mo

    mod = mod_ref[tile // tiles_per_seq]
    shift1, scale1 = mod[:, 0:d], mod[:, d:2 * d]
    x = xbuf[slot].reshape(t, d)
    hb_buf[...] = _mod_norm(x, g_ref[...] * (1.0 + scale1), shift1).astype(jnp.bfloat16)

    def in_proj(j):
        z_buf[j] = _bdot(hb_buf[...], _unpack_rows(w_in_ref[:, j * d:(j + 1) * d]))

    in_proj(0)
    in_proj(3)
    in_proj(2)

    taps = [cw_ref[k:k + 1, :] for k in range(kw)]
    prev_tail = jnp.where(seq_start, 0.0, xr_tail[...])
    xc_blocks = []
    for n in range(n_blocks):
        blk = z_buf[0, n * SGU_BLOCK:(n + 1) * SGU_BLOCK, :]
        xc_blocks.append(_causal_conv_block(blk, prev_tail, taps, cb_ref[...]))
        prev_tail = blk[SGU_BLOCK - halo:]
    xr_tail[...] = prev_tail
    xc = jnp.concatenate(xc_blocks, axis=0)
    xcb = xc.astype(jnp.bfloat16)
    pre = [_bdot(xcb[:, i * hd:(i + 1) * hd], _unpack_rows(wax_ref[i])) for i in range(RNN_HEADS)]

    in_proj(1)
    in_proj(4)
    in_proj(5)

    r = _sigmoid(jnp.concatenate([p[:, :hd] for p in pre], axis=1) + ba_ref[...])
    gi = _sigmoid(jnp.concatenate([p[:, hd:] for p in pre], axis=1) + bx_ref[...])
    lam = lam_ref[...]
    log_sig = jnp.minimum(lam, 0.0) - jnp.log1p(jnp.exp(-jnp.abs(lam)))
    log_a = r * (LRU_C * log_sig)
    a = jnp.exp(log_a)
    mult = jnp.sqrt(-jnp.tanh(log_a) * (a * a + 1.0))
    u = mult * (gi * xc)
    state = jnp.where(seq_start, 0.0, h_carry[...])
    hs_blocks = []
    for n in range(n_blocks):
        rows = slice(n * SGU_BLOCK, (n + 1) * SGU_BLOCK)
        hs, state = _linear_scan_block(a[rows], u[rows], state)
        hs_blocks.append(hs)
    h_carry[...] = state
    ya_pre = (jnp.concatenate(hs_blocks, axis=0) * _gelu(z_buf[1])).astype(jnp.bfloat16)

    gu = _gelu(z_buf[2])
    gv = _gelu(z_buf[3])
    mu = jnp.mean(gv, axis=-1, keepdims=True)
    gc = gv - mu
    var = jnp.mean(gc * gc, axis=-1, keepdims=True)
    vb = ((gc * jax.lax.rsqrt(var + EPS)) * lng_ref[...] + lnb_ref[...]).astype(jnp.bfloat16)
    ri = jax.lax.broadcasted_iota(jnp.int32, (SGU_BLOCK, SGU_BLOCK), 0)
    ci = jax.lax.broadcasted_iota(jnp.int32, (SGU_BLOCK, SGU_BLOCK), 1)
    time_of = lambda i: (i & (V7X_SUBLANES - 1)) * SEG + (i >> 3)
    causal = time_of(ci) <= time_of(ri)
    wm = [jnp.where(causal, ws_ref[g], 0.0).astype(jnp.bfloat16) for g in range(SGU_GROUPS)]
    blocks = []
    for n in range(n_blocks):
        rows = slice(n * SGU_BLOCK, (n + 1) * SGU_BLOCK)
        mixed = jnp.concatenate(
            [_bdot(wm[g], vb[rows, g * gd:(g + 1) * gd]) for g in range(SGU_GROUPS)], axis=1)
        blocks.append(mixed + sb_ref[...])
    yb_pre = (gu * jnp.concatenate(blocks, axis=0)).astype(jnp.bfloat16)

    ya = _bdot(ya_pre, _unpack_rows(wba_ref[...]))
    yb = _bdot(yb_pre, _unpack_rows(wbb_ref[...]))
    merged = _sigmoid(z_buf[4]) * ya + _sigmoid(z_buf[5]) * yb
    merged_buf[...] = merged.astype(jnp.bfloat16)


def _mixer(x, mod3, norm_g, w_in, conv_w, conv_b, wax, b_a, b_x, lam, ln_g, ln_b, w_s, sgu_bias,
           w_ba, w_bb, w_out):
    bsz, seq, d = x.shape
    t = TIME_TILE
    n_t = seq // t
    n_tiles = bsz * n_t
    consts = [mod3, norm_g, w_in, conv_w, conv_b, wax, b_a, b_x, lam, ln_g, ln_b, w_s, sgu_bias,
              w_ba, w_bb, w_out]
    halo = (conv_w.shape[0] - 1) * V7X_SUBLANES
    return pl.pallas_call(
        functools.partial(_mixer_kernel, tiles_per_seq=n_t, n_tiles=n_tiles),
        grid=(n_tiles + 1,),
        in_specs=[pl.BlockSpec(memory_space=pl.ANY)] + [_const_spec(a.shape) for a in consts],
        out_specs=pl.BlockSpec((1, t, d), lambda g: (jnp.maximum(g - 1, 0), 0, 0)),
        out_shape=jax.ShapeDtypeStruct((n_tiles, t, d), jnp.float32),
        scratch_shapes=[
            pltpu.VMEM((3, t // V7X_SUBLANES, V7X_SUBLANES, d), jnp.float32),
            pltpu.SemaphoreType.DMA((3,)),
            pltpu.VMEM((6, t, d), jnp.float32),
            pltpu.VMEM((t, d), jnp.bfloat16),
            pltpu.VMEM((halo, d), jnp.float32),
            pltpu.VMEM((1, d), jnp.float32),
            pltpu.VMEM((t, d), jnp.bfloat16),
        ],
        compiler_params=pltpu.CompilerParams(
            dimension_semantics=("arbitrary",), vmem_limit_bytes=VMEM_LIMIT_BYTES),
        name="mixer",
    )(x.reshape(bsz * seq, 1, d), *consts)


def _ffn_kernel(x0_ref, xnext_ref, xres_ref, mod_ref, g_ref, wup_ref, cw_ref, cb_ref,
                wdn_ref, gf_ref, out_hbm, up_buf, tail_buf, hb_cur, hb_next, acc_buf, p_buf,
                obuf, sems, *, final_norm, tiles_per_seq, n_tiles, n_chunks):
    t = xnext_ref.shape[1]
    d = xnext_ref.shape[2]
    cw = wdn_ref.shape[0] * 2 // n_chunks
    kw = cw_ref.shape[0]
    halo = (kw - 1) * V7X_SUBLANES
    n_blocks = t // SGU_BLOCK
    step = pl.program_id(0)
    seq_start = jax.lax.rem(step, tiles_per_seq) == 0
    g_norm = g_ref[...]
    oslot = jax.lax.rem(step + 1, 2)

    def out_copies(tile_idx, slot_idx):
        return _tile_row_copies(out_hbm, obuf.at[slot_idx], tile_idx * t, sems.at[slot_idx], False)

    def norm_rows(x, b):
        mod = mod_ref[b]
        return _mod_norm(x, g_norm * (1.0 + mod[:, 4 * d:5 * d]),
                         mod[:, 3 * d:4 * d]).astype(jnp.bfloat16)

    @pl.when(step == 0)
    def _():
        hb_next[...] = norm_rows(x0_ref[0], 0)
        acc_buf[...] = jnp.zeros_like(acc_buf)

    @pl.when(step >= 3)
    def _():
        for cp in out_copies(step - 3, oslot):
            cp.wait()

    hb_cur[...] = hb_next[...]

    b_res = jnp.maximum(step - 1, 0) // tiles_per_seq
    b_next = jnp.minimum(step + 1, n_tiles - 1) // tiles_per_seq
    gate2 = mod_ref[b_res][:, 5 * d:6 * d]
    for r in range(0, t, FFN_ROW_PIECE):
        rows = slice(r, r + FFN_ROW_PIECE)
        y = xres_ref[0, rows, :] + gate2 * acc_buf[rows, :]
        if final_norm:
            y = (y * jax.lax.rsqrt(jnp.mean(y * y, axis=-1, keepdims=True) + EPS)) * gf_ref[...]
        groups = slice(r // V7X_SUBLANES, (r + FFN_ROW_PIECE) // V7X_SUBLANES)
        obuf[oslot, groups] = y.reshape(FFN_ROW_PIECE // V7X_SUBLANES, V7X_SUBLANES, d)
        hb_next[rows, :] = norm_rows(xnext_ref[0, rows, :], b_next)

    def down_group(g):
        lo, hi = g * FFN_DOWN_GROUP * cw, (g + 1) * FFN_DOWN_GROUP * cw
        part = _bdot(p_buf[:, lo:hi], _unpack_rows(wdn_ref[lo // 2:hi // 2, :]))
        if g == 0:
            acc_buf[...] = part
        else:
            acc_buf[...] += part

    for k in range(n_chunks):
        cols = slice(k * 2 * cw, (k + 1) * 2 * cw)
        up_buf[k] = _bdot(hb_cur[...], _unpack_rows(wup_ref[:, cols]))
        taps = [cw_ref[j:j + 1, cols] for j in range(kw)]
        prev_tail = jnp.where(seq_start, 0.0, tail_buf[k])
        y_blocks = []
        for n in range(n_blocks):
            blk = up_buf[k, n * SGU_BLOCK:(n + 1) * SGU_BLOCK, :]
            y_blocks.append(_causal_conv_block(blk, prev_tail, taps, cb_ref[:, cols]))
            prev_tail = blk[SGU_BLOCK - halo:]
        tail_buf[k] = prev_tail
        y = jnp.concatenate(y_blocks, axis=0)
        p_buf[:, k * cw:(k + 1) * cw] = (_gelu(y[:, :cw]) * y[:, cw:]).astype(jnp.bfloat16)
        for g, after in enumerate(FFN_DOWN_AFTER):
            if after == k:
                down_group(g)

    @pl.when(step >= 1)
    def _():
        for cp in out_copies(step - 1, oslot):
            cp.start()

    @pl.when(step == n_tiles)
    def _():
        for cp in out_copies(step - 2, 1 - oslot):
            cp.wait()
        for cp in out_copies(step - 1, oslot):
            cp.wait()


def _ffn(x_tiles, mod3, norm_g, w_up, conv_w, conv_b, w_down, norm_final_g, final_norm, out_shape):
    n_tiles, t, d = x_tiles.shape
    bsz, seq, _ = out_shape
    dff = w_down.shape[0]
    nc = FFN_CHUNKS
    cw = dff // nc
    n_t = seq // t
    halo = (conv_w.shape[0] - 1) * V7X_SUBLANES
    perm = lambda a: a.reshape(-1, 2, nc, cw).transpose(0, 2, 1, 3).reshape(-1, 2 * dff)
    wup = _pack_rows(w_up, block_cols=cw, src_block=lambda j: (j % 2) * nc + j // 2)
    consts = [mod3, norm_g, wup, perm(conv_w), perm(conv_b), _pack_rows(w_down), norm_final_g]
    tile_res = lambda g: jnp.maximum(g - 1, 0)
    out = pl.pallas_call(
        functools.partial(_ffn_kernel, final_norm=final_norm, tiles_per_seq=n_t,
                          n_tiles=n_tiles, n_chunks=nc),
        grid=(n_tiles + 1,),
        in_specs=[
            pl.BlockSpec((1, t, d), lambda g: (0, 0, 0), pipeline_mode=pl.Buffered(1)),
            pl.BlockSpec((1, t, d), lambda g: (jnp.minimum(g + 1, n_tiles - 1), 0, 0)),
            pl.BlockSpec((1, t, d), lambda g: (tile_res(g), 0, 0)),
        ] + [_const_spec(a.shape) for a in consts],
        out_specs=pl.BlockSpec(memory_space=pl.ANY),
        out_shape=jax.ShapeDtypeStruct((bsz * seq, 1, d), jnp.float32),
        scratch_shapes=[
            pltpu.VMEM((nc, t, 2 * cw), jnp.float32),
            pltpu.VMEM((nc, halo, 2 * cw), jnp.float32),
            pltpu.VMEM((t, d), jnp.bfloat16),
            pltpu.VMEM((t, d), jnp.bfloat16),
            pltpu.VMEM((t, d), jnp.float32),
            pltpu.VMEM((t, dff), jnp.bfloat16),
            pltpu.VMEM((2, t // V7X_SUBLANES, V7X_SUBLANES, d), jnp.float32),
            pltpu.SemaphoreType.DMA((2,)),
        ],
        compiler_params=pltpu.CompilerParams(
            dimension_semantics=("arbitrary",), vmem_limit_bytes=VMEM_LIMIT_BYTES),
        name="ffn",
    )(x_tiles, x_tiles, x_tiles, *consts)
    return out.reshape(bsz, seq, d)


def kernel(x, c, w_ada, b_ada, norm_mix_g, w_in, rnn_conv_w, rnn_conv_b, lru_w_a, lru_b_a, lru_w_x, lru_b_x, lru_lambda, sgu_ln_g, sgu_ln_b, sgu_w_s, sgu_b_s, w_branch_a, w_branch_b, w_out, norm_ffn_g, w_up, ffn_conv_w, ffn_conv_b, w_down, norm_final_g):
    depth = w_in.shape[0]
    d = x.shape[-1]
    hd = d // RNN_HEADS
    row = lambda v: v.reshape(1, -1)
    time_of = _block_time_index()
    for l in range(depth):
        mod3 = _adaln(c, w_ada[l], b_ada[l]).reshape(x.shape[0], 1, N_MOD * d)
        wax = jnp.concatenate([lru_w_a[l], lru_w_x[l]], axis=-1)
        wax = _pack_rows(wax.reshape(d, 2 * hd)).reshape(RNN_HEADS, hd // 2, 2 * hd)
        w_s = sgu_w_s[l][:, time_of, :][:, :, time_of]
        sgu_bias = jnp.repeat(sgu_b_s[l].T, d // SGU_GROUPS, axis=1)[time_of, :]
        x1 = _mixer(x, mod3, row(norm_mix_g[l]), _pack_rows(w_in[l]), rnn_conv_w[l],
                    row(rnn_conv_b[l]), wax, row(lru_b_a[l]), row(lru_b_x[l]), row(lru_lambda[l]),
                    row(sgu_ln_g[l]), row(sgu_ln_b[l]), w_s, sgu_bias,
                    _pack_rows(w_branch_a[l]), _pack_rows(w_branch_b[l]), _pack_rows(w_out[l]))
        x = _ffn(x1, mod3, row(norm_ffn_g[l]), w_up[l], ffn_conv_w[l],
                 ffn_conv_b[l], w_down[l], row(norm_final_g),
                 final_norm=(l == depth - 1), out_shape=x.shape)
    return x
```

```python
import functools

import jax
import jax.numpy as jnp
import numpy as np
from jax.experimental import pallas as pl
from jax.experimental.pallas import tpu as pltpu

EPS = 1e-6
LRU_C = 8.0
RNN_HEADS = 8
SGU_GROUPS = 8
SGU_BLOCK = 128
N_MOD = 6

V7X_SUBLANES = 8
TIME_TILE = 256
SEG = SGU_BLOCK // V7X_SUBLANES
FFN_CHUNKS = 12
FFN_ROW_PIECE = 64
FFN_DOWN_GROUP = 3
FFN_DOWN_AFTER = (4, 7, 11, 11)
ADALN_COL_BLOCK = 1536
PACK_COL_BLOCK = 512
VMEM_LIMIT_BYTES = 56 * 1024 * 1024

_GELU_C0 = 0.7978845608028654
_GELU_C1 = 0.044715


def _gelu(x):
    inner = x * (_GELU_C0 + (_GELU_C0 * _GELU_C1) * (x * x))
    hx = 0.5 * x
    return hx + hx * jnp.tanh(inner)


def _sigmoid(x):
    return 0.5 * jnp.tanh(0.5 * x) + 0.5


def _bdot(a, b):
    return jnp.dot(a, b, preferred_element_type=jnp.float32)


def _unpack_rows(w):
    return pltpu.bitcast(w, jnp.bfloat16)


def _mod_norm(x, gs, shift):
    r = jax.lax.rsqrt(jnp.mean(x * x, axis=-1, keepdims=True) + EPS)
    return (x * r) * gs + shift


def _block_time_index():
    rho = np.arange(SGU_BLOCK)
    return (rho % V7X_SUBLANES) * SEG + rho // V7X_SUBLANES


def _sublane_iota(cols):
    return jax.lax.broadcasted_iota(jnp.int32, (V7X_SUBLANES, cols), 0)


def _boundary_groups(blk_tail, prev_tail):
    sub = _sublane_iota(blk_tail.shape[1])
    groups = []
    for i in range(0, blk_tail.shape[0], V7X_SUBLANES):
        cur = pltpu.roll(blk_tail[i:i + V7X_SUBLANES], 1, 0)
        prv = pltpu.roll(prev_tail[i:i + V7X_SUBLANES], 1, 0)
        groups.append(jnp.where(sub == 0, prv, cur))
    return jnp.concatenate(groups, axis=0)


def _causal_conv_block(x_blk, prev_tail, taps, bias):
    halo = prev_tail.shape[0]
    n = x_blk.shape[0]
    ext = jnp.concatenate([_boundary_groups(x_blk[n - halo:], prev_tail), x_blk], axis=0)
    y = bias + ext[0:n] * taps[0]
    for k in range(1, len(taps)):
        y = y + ext[k * V7X_SUBLANES:k * V7X_SUBLANES + n] * taps[k]
    return y


def _sublane_scan(a, b, carry):
    sub = _sublane_iota(a.shape[1])
    b = b + jnp.where(sub == 0, a * carry, 0.0)
    k = 1
    while k < V7X_SUBLANES:
        am = jnp.where(sub < k, 0.0, a)
        b = b + am * pltpu.roll(b, k, 0)
        a = am * pltpu.roll(a, k, 0)
        k *= 2
    return b


def _linear_scan_block(a, u, carry):
    n_groups = a.shape[0] // V7X_SUBLANES
    grp = lambda v, g: v[g * V7X_SUBLANES:(g + 1) * V7X_SUBLANES]
    hs, ps = [grp(u, 0)], [grp(a, 0)]
    for g in range(1, n_groups):
        hs.append(grp(a, g) * hs[-1] + grp(u, g))
        ps.append(grp(a, g) * ps[-1])
    seg_end = _sublane_scan(ps[-1], hs[-1], carry)
    sub = _sublane_iota(a.shape[1])
    seg_in = jnp.where(sub == 0, carry, pltpu.roll(seg_end, 1, 0))
    h = jnp.concatenate([hh + pp * seg_in for hh, pp in zip(hs, ps)], axis=0)
    return h, seg_end[V7X_SUBLANES - 1:V7X_SUBLANES, :]


def _tile_row_copies(hbm_rows, vmem_tile, first_row, sem, to_vmem):
    copies = []
    first_seg = first_row // SEG
    for n in range(vmem_tile.shape[0] // SEG):
        for g in range(SEG):
            hbm = hbm_rows.at[pl.ds(first_seg + n * V7X_SUBLANES, V7X_SUBLANES), g, :]
            vmem = vmem_tile.at[n * SEG + g]
            src, dst = (hbm, vmem) if to_vmem else (vmem, hbm)
            copies.append(pltpu.make_async_copy(src, dst, sem))
    return copies


def _pack_kernel(w_ref, o_ref):
    o_ref[...] = pltpu.bitcast(w_ref[...].astype(jnp.bfloat16), jnp.uint32)


def _pack_rows(w, block_cols=PACK_COL_BLOCK, src_block=None):
    k, n = w.shape
    block_cols = min(block_cols, n)
    src = src_block or (lambda j: j)
    return pl.pallas_call(
        _pack_kernel,
        grid=(n // block_cols,),
        in_specs=[pl.BlockSpec((k, block_cols), lambda j: (0, src(j)))],
        out_specs=pl.BlockSpec((k // 2, block_cols), lambda j: (0, j)),
        out_shape=jax.ShapeDtypeStruct((k // 2, n), jnp.uint32),
        compiler_params=pltpu.CompilerParams(
            dimension_semantics=("arbitrary",), vmem_limit_bytes=VMEM_LIMIT_BYTES),
        name="pack_bf16",
    )(w)


def _adaln_kernel(ct_ref, w_ref, b_ref, o_ref):
    ct = ct_ref[...]
    act = ct * _sigmoid(ct)
    w = w_ref[...]
    rows = []
    for b in range(ct.shape[1]):
        rows.append(jnp.sum(act[:, b:b + 1] * w, axis=0, keepdims=True))
    o_ref[...] = jnp.concatenate(rows, axis=0) + b_ref[...]


def _adaln(c, w_ada, b_ada):
    bsz, d = c.shape
    n = w_ada.shape[1]
    bn = ADALN_COL_BLOCK
    return pl.pallas_call(
        _adaln_kernel,
        grid=(n // bn,),
        in_specs=[
            pl.BlockSpec((d, bsz), lambda j: (0, 0)),
            pl.BlockSpec((d, bn), lambda j: (0, j)),
            pl.BlockSpec((1, bn), lambda j: (0, j)),
        ],
        out_specs=pl.BlockSpec((bsz, bn), lambda j: (0, j)),
        out_shape=jax.ShapeDtypeStruct((bsz, n), jnp.float32),
        compiler_params=pltpu.CompilerParams(
            dimension_semantics=("arbitrary",), vmem_limit_bytes=VMEM_LIMIT_BYTES),
        name="adaln",
    )(c.T, w_ada, b_ada.reshape(1, n))


def _const_spec(shape):
    zeros = (0,) * len(shape)
    return pl.BlockSpec(shape, lambda g: zeros, pipeline_mode=pl.Buffered(1))


def _mixer_kernel(x_hbm, mod_ref, g_ref, w_in_ref, cw_ref, cb_ref, wax_ref, ba_ref, bx_ref,
                  lam_ref, lng_ref, lnb_ref, ws_ref, sb_ref, wba_ref, wbb_ref, wo_ref,
                  o_ref, xbuf, sems, z_buf, hb_buf, xr_tail, h_carry, merged_buf,
                  *, tiles_per_seq, n_tiles):
    t, d = o_ref.shape[1], o_ref.shape[2]
    hd = d // RNN_HEADS
    gd = d // SGU_GROUPS
    kw = cw_ref.shape[0]
    halo = (kw - 1) * V7X_SUBLANES
    n_blocks = t // SGU_BLOCK
    step = pl.program_id(0)
    tile = jnp.minimum(step, n_tiles - 1)
    slot = jax.lax.rem(tile, 3)
    slot_prev = jax.lax.rem(step + 2, 3)
    seq_start = jax.lax.rem(tile, tiles_per_seq) == 0

    def x_copies(tile_idx, slot_idx):
        return _tile_row_copies(x_hbm, xbuf.at[slot_idx], tile_idx * t, sems.at[slot_idx], True)

    @pl.when(step == 0)
    def _():
        for cp in x_copies(0, 0):
            cp.start()
        merged_buf[...] = jnp.zeros_like(merged_buf)
        xbuf[2] = jnp.zeros(xbuf.shape[1:], jnp.float32)

    @pl.when(step < n_tiles)
    def _():
        for cp in x_copies(step, slot):
            cp.wait()

    @pl.when(step + 1 < n_tiles)
    def _():
        for cp in x_copies(step + 1, jax.lax.rem(step + 1, 3)):
            cp.start()

    gate1_prev = mod_ref[jnp.maximum(step - 1, 0) // tiles_per_seq][:, 2 * d:3 * d]
    mo = _bdot(merged_buf[...], _unpack_rows(wo_ref[...]))
    o_ref[0] = xbuf[slot_prev].reshape(t, d) + gate1_prev * mo

    mod = mod_ref[tile // tiles_per_seq]
    shift1, scale1 = mod[:, 0:d], mod[:, d:2 * d]
    x = xbuf[slot].reshape(t, d)
    hb_buf[...] = _mod_norm(x, g_ref[...] * (1.0 + scale1), shift1).astype(jnp.bfloat16)

    def in_proj(j):
        z_buf[j] = _bdot(hb_buf[...], _unpack_rows(w_in_ref[:, j * d:(j + 1) * d]))

    in_proj(0)
    in_proj(3)
    in_proj(2)

    taps = [cw_ref[k:k + 1, :] for k in range(kw)]
    prev_tail = jnp.where(seq_start, 0.0, xr_tail[...])
    xc_blocks = []
    for n in range(n_blocks):
        blk = z_buf[0, n * SGU_BLOCK:(n + 1) * SGU_BLOCK, :]
        xc_blocks.append(_causal_conv_block(blk, prev_tail, taps, cb_ref[...]))
        prev_tail = blk[SGU_BLOCK - halo:]
    xr_tail[...] = prev_tail
    xc = jnp.concatenate(xc_blocks, axis=0)
    xcb = xc.astype(jnp.bfloat16)
    pre = [_bdot(xcb[:, i * hd:(i + 1) * hd], _unpack_rows(wax_ref[i])) for i in range(RNN_HEADS)]

    in_proj(1)
    in_proj(4)
    in_proj(5)

    r = _sigmoid(jnp.concatenate([p[:, :hd] for p in pre], axis=1) + ba_ref[...])
    gi = _sigmoid(jnp.concatenate([p[:, hd:] for p in pre], axis=1) + bx_ref[...])
    lam = lam_ref[...]
    log_sig = jnp.minimum(lam, 0.0) - jnp.log1p(jnp.exp(-jnp.abs(lam)))
    log_a = r * (LRU_C * log_sig)
    a = jnp.exp(log_a)
    mult = jnp.sqrt(-jnp.tanh(log_a) * (a * a + 1.0))
    u = mult * (gi * xc)
    state = jnp.where(seq_start, 0.0, h_carry[...])
    hs_blocks = []
    for n in range(n_blocks):
        rows = slice(n * SGU_BLOCK, (n + 1) * SGU_BLOCK)
        hs, state = _linear_scan_block(a[rows], u[rows], state)
        hs_blocks.append(hs)
    h_carry[...] = state
    ya_pre = (jnp.concatenate(hs_blocks, axis=0) * _gelu(z_buf[1])).astype(jnp.bfloat16)

    gu = _gelu(z_buf[2])
    gv = _gelu(z_buf[3])
    mu = jnp.mean(gv, axis=-1, keepdims=True)
    gc = gv - mu
    var = jnp.mean(gc * gc, axis=-1, keepdims=True)
    vb = ((gc * jax.lax.rsqrt(var + EPS)) * lng_ref[...] + lnb_ref[...]).astype(jnp.bfloat16)
    ri = jax.lax.broadcasted_iota(jnp.int32, (SGU_BLOCK, SGU_BLOCK), 0)
    ci = jax.lax.broadcasted_iota(jnp.int32, (SGU_BLOCK, SGU_BLOCK), 1)
    time_of = lambda i: (i & (V7X_SUBLANES - 1)) * SEG + (i >> 3)
    causal = time_of(ci) <= time_of(ri)
    wm = [jnp.where(causal, ws_ref[g], 0.0).astype(jnp.bfloat16) for g in range(SGU_GROUPS)]
    blocks = []
    for n in range(n_blocks):
        rows = slice(n * SGU_BLOCK, (n + 1) * SGU_BLOCK)
        mixed = jnp.concatenate(
            [_bdot(wm[g], vb[rows, g * gd:(g + 1) * gd]) for g in range(SGU_GROUPS)], axis=1)
        blocks.append(mixed + sb_ref[...])
    yb_pre = (gu * jnp.concatenate(blocks, axis=0)).astype(jnp.bfloat16)

    ya = _bdot(ya_pre, _unpack_rows(wba_ref[...]))
    yb = _bdot(yb_pre, _unpack_rows(wbb_ref[...]))
    merged = _sigmoid(z_buf[4]) * ya + _sigmoid(z_buf[5]) * yb
    merged_buf[...] = merged.astype(jnp.bfloat16)


def _mixer(x, mod3, norm_g, w_in, conv_w, conv_b, wax, b_a, b_x, lam, ln_g, ln_b, w_s, sgu_bias,
           w_ba, w_bb, w_out):
    bsz, seq, d = x.shape
    t = TIME_TILE
    n_t = seq // t
    n_tiles = bsz * n_t
    consts = [mod3, norm_g, w_in, conv_w, conv_b, wax, b_a, b_x, lam, ln_g, ln_b, w_s, sgu_bias,
              w_ba, w_bb, w_out]
    halo = (conv_w.shape[0] - 1) * V7X_SUBLANES
    return pl.pallas_call(
        functools.partial(_mixer_kernel, tiles_per_seq=n_t, n_tiles=n_tiles),
        grid=(n_tiles + 1,),
        in_specs=[pl.BlockSpec(memory_space=pl.ANY)] + [_const_spec(a.shape) for a in consts],
        out_specs=pl.BlockSpec((1, t, d), lambda g: (jnp.maximum(g - 1, 0), 0, 0)),
        out_shape=jax.ShapeDtypeStruct((n_tiles, t, d), jnp.float32),
        scratch_shapes=[
            pltpu.VMEM((3, t // V7X_SUBLANES, V7X_SUBLANES, d), jnp.float32),
            pltpu.SemaphoreType.DMA((3,)),
            pltpu.VMEM((6, t, d), jnp.float32),
            pltpu.VMEM((t, d), jnp.bfloat16),
            pltpu.VMEM((halo, d), jnp.float32),
            pltpu.VMEM((1, d), jnp.float32),
            pltpu.VMEM((t, d), jnp.bfloat16),
        ],
        compiler_params=pltpu.CompilerParams(
            dimension_semantics=("arbitrary",), vmem_limit_bytes=VMEM_LIMIT_BYTES),
        name="mixer",
    )(x.reshape(bsz * seq // SEG, SEG, d), *consts)


def _ffn_kernel(x0_ref, xnext_ref, xres_ref, mod_ref, g_ref, wup_ref, cw_ref, cb_ref,
                wdn_ref, gf_ref, out_hbm, up_buf, tail_buf, hb_cur, hb_next, acc_buf, p_buf,
                obuf, sems, *, final_norm, tiles_per_seq, n_tiles, n_chunks):
    t = xnext_ref.shape[1]
    d = xnext_ref.shape[2]
    cw = wdn_ref.shape[0] * 2 // n_chunks
    kw = cw_ref.shape[0]
    halo = (kw - 1) * V7X_SUBLANES
    n_blocks = t // SGU_BLOCK
    step = pl.program_id(0)
    seq_start = jax.lax.rem(step, tiles_per_seq) == 0
    g_norm = g_ref[...]
    oslot = jax.lax.rem(step + 1, 2)

    def out_copies(tile_idx, slot_idx):
        return _tile_row_copies(out_hbm, obuf.at[slot_idx], tile_idx * t, sems.at[slot_idx], False)

    def norm_rows(x, b):
        mod = mod_ref[b]
        return _mod_norm(x, g_norm * (1.0 + mod[:, 4 * d:5 * d]),
                         mod[:, 3 * d:4 * d]).astype(jnp.bfloat16)

    @pl.when(step == 0)
    def _():
        hb_next[...] = norm_rows(x0_ref[0], 0)
        acc_buf[...] = jnp.zeros_like(acc_buf)

    @pl.when(step >= 3)
    def _():
        for cp in out_copies(step - 3, oslot):
            cp.wait()

    hb_cur[...] = hb_next[...]

    b_res = jnp.maximum(step - 1, 0) // tiles_per_seq
    b_next = jnp.minimum(step + 1, n_tiles - 1) // tiles_per_seq
    gate2 = mod_ref[b_res][:, 5 * d:6 * d]
    for r in range(0, t, FFN_ROW_PIECE):
        rows = slice(r, r + FFN_ROW_PIECE)
        y = xres_ref[0, rows, :] + gate2 * acc_buf[rows, :]
        if final_norm:
            y = (y * jax.lax.rsqrt(jnp.mean(y * y, axis=-1, keepdims=True) + EPS)) * gf_ref[...]
        groups = slice(r // V7X_SUBLANES, (r + FFN_ROW_PIECE) // V7X_SUBLANES)
        obuf[oslot, groups] = y.reshape(FFN_ROW_PIECE // V7X_SUBLANES, V7X_SUBLANES, d)
        hb_next[rows, :] = norm_rows(xnext_ref[0, rows, :], b_next)

    def down_group(g):
        lo, hi = g * FFN_DOWN_GROUP * cw, (g + 1) * FFN_DOWN_GROUP * cw
        part = _bdot(p_buf[:, lo:hi], _unpack_rows(wdn_ref[lo // 2:hi // 2, :]))
        if g == 0:
            acc_buf[...] = part
        else:
            acc_buf[...] += part

    for k in range(n_chunks):
        cols = slice(k * 2 * cw, (k + 1) * 2 * cw)
        up_buf[k] = _bdot(hb_cur[...], _unpack_rows(wup_ref[:, cols]))
        taps = [cw_ref[j:j + 1, cols] for j in range(kw)]
        prev_tail = jnp.where(seq_start, 0.0, tail_buf[k])
        y_blocks = []
        for n in range(n_blocks):
            blk = up_buf[k, n * SGU_BLOCK:(n + 1) * SGU_BLOCK, :]
            y_blocks.append(_causal_conv_block(blk, prev_tail, taps, cb_ref[:, cols]))
            prev_tail = blk[SGU_BLOCK - halo:]
        tail_buf[k] = prev_tail
        y = jnp.concatenate(y_blocks, axis=0)
        p_buf[:, k * cw:(k + 1) * cw] = (_gelu(y[:, :cw]) * y[:, cw:]).astype(jnp.bfloat16)
        for g, after in enumerate(FFN_DOWN_AFTER):
            if after == k:
                down_group(g)

    @pl.when(step >= 1)
    def _():
        for cp in out_copies(step - 1, oslot):
            cp.start()

    @pl.when(step == n_tiles)
    def _():
        for cp in out_copies(step - 2, 1 - oslot):
            cp.wait()
        for cp in out_copies(step - 1, oslot):
            cp.wait()


def _ffn(x_tiles, mod3, norm_g, w_up, conv_w, conv_b, w_down, norm_final_g, final_norm, out_shape):
    n_tiles, t, d = x_tiles.shape
    bsz, seq, _ = out_shape
    dff = w_down.shape[0]
    nc = FFN_CHUNKS
    cw = dff // nc
    n_t = seq // t
    halo = (conv_w.shape[0] - 1) * V7X_SUBLANES
    perm = lambda a: a.reshape(-1, 2, nc, cw).transpose(0, 2, 1, 3).reshape(-1, 2 * dff)
    wup = _pack_rows(w_up, block_cols=cw, src_block=lambda j: (j % 2) * nc + j // 2)
    consts = [mod3, norm_g, wup, perm(conv_w), perm(conv_b), _pack_rows(w_down), norm_final_g]
    tile_res = lambda g: jnp.maximum(g - 1, 0)
    out = pl.pallas_call(
        functools.partial(_ffn_kernel, final_norm=final_norm, tiles_per_seq=n_t,
                          n_tiles=n_tiles, n_chunks=nc),
        grid=(n_tiles + 1,),
        in_specs=[
            pl.BlockSpec((1, t, d), lambda g: (0, 0, 0), pipeline_mode=pl.Buffered(1)),
            pl.BlockSpec((1, t, d), lambda g: (jnp.minimum(g + 1, n_tiles - 1), 0, 0)),
            pl.BlockSpec((1, t, d), lambda g: (tile_res(g), 0, 0)),
        ] + [_const_spec(a.shape) for a in consts],
        out_specs=pl.BlockSpec(memory_space=pl.ANY),
        out_shape=jax.ShapeDtypeStruct((bsz * seq // SEG, SEG, d), jnp.float32),
        scratch_shapes=[
            pltpu.VMEM((nc, t, 2 * cw), jnp.float32),
            pltpu.VMEM((nc, halo, 2 * cw), jnp.float32),
            pltpu.VMEM((t, d), jnp.bfloat16),
            pltpu.VMEM((t, d), jnp.bfloat16),
            pltpu.VMEM((t, d), jnp.float32),
            pltpu.VMEM((t, dff), jnp.bfloat16),
            pltpu.VMEM((2, t // V7X_SUBLANES, V7X_SUBLANES, d), jnp.float32),
            pltpu.SemaphoreType.DMA((2,)),
        ],
        compiler_params=pltpu.CompilerParams(
            dimension_semantics=("arbitrary",), vmem_limit_bytes=VMEM_LIMIT_BYTES),
        name="ffn",
    )(x_tiles, x_tiles, x_tiles, *consts)
    return out.reshape(bsz, seq, d)


def kernel(x, c, w_ada, b_ada, norm_mix_g, w_in, rnn_conv_w, rnn_conv_b, lru_w_a, lru_b_a, lru_w_x, lru_b_x, lru_lambda, sgu_ln_g, sgu_ln_b, sgu_w_s, sgu_b_s, w_branch_a, w_branch_b, w_out, norm_ffn_g, w_up, ffn_conv_w, ffn_conv_b, w_down, norm_final_g):
    depth = w_in.shape[0]
    d = x.shape[-1]
    hd = d // RNN_HEADS
    row = lambda v: v.reshape(1, -1)
    time_of = _block_time_index()
    for l in range(depth):
        mod3 = _adaln(c, w_ada[l], b_ada[l]).reshape(x.shape[0], 1, N_MOD * d)
        wax = jnp.concatenate([lru_w_a[l], lru_w_x[l]], axis=-1)
        wax = _pack_rows(wax.reshape(d, 2 * hd)).reshape(RNN_HEADS, hd // 2, 2 * hd)
        w_s = sgu_w_s[l][:, time_of, :][:, :, time_of]
        sgu_bias = jnp.repeat(sgu_b_s[l].T, d // SGU_GROUPS, axis=1)[time_of, :]
        x1 = _mixer(x, mod3, row(norm_mix_g[l]), _pack_rows(w_in[l]), rnn_conv_w[l],
                    row(rnn_conv_b[l]), wax, row(lru_b_a[l]), row(lru_b_x[l]), row(lru_lambda[l]),
                    row(sgu_ln_g[l]), row(sgu_ln_b[l]), w_s, sgu_bias,
                    _pack_rows(w_branch_a[l]), _pack_rows(w_branch_b[l]), _pack_rows(w_out[l]))
        x = _ffn(x1, mod3, row(norm_ffn_g[l]), w_up[l], ffn_conv_w[l],
                 ffn_conv_b[l], w_down[l], row(norm_final_g),
                 final_norm=(l == depth - 1), out_shape=x.shape)
    return x
```

```python
import functools

import jax
import jax.numpy as jnp
import numpy as np
from jax.experimental import pallas as pl
from jax.experimental.pallas import tpu as pltpu

EPS = 1e-6
LRU_C = 8.0
RNN_HEADS = 8
SGU_GROUPS = 8
SGU_BLOCK = 128
N_MOD = 6

V7X_SUBLANES = 8
TIME_TILE = 256
SEG = SGU_BLOCK // V7X_SUBLANES
FFN_CHUNKS = 12
FFN_ROW_PIECE = 64
FFN_HEAD_CHUNKS = 2
FFN_DOWN_GROUP = 1
FFN_DOWN_AFTER = (2, 3, 4, 5, 6, 7, 8, 9, 10, 11, 11, 11)
ADALN_COL_BLOCK = 1536
PACK_COL_BLOCK = 512
VMEM_LIMIT_BYTES = 56 * 1024 * 1024

_GELU_C0 = 0.7978845608028654
_GELU_C1 = 0.044715


def _gelu(x):
    inner = x * (_GELU_C0 + (_GELU_C0 * _GELU_C1) * (x * x))
    hx = 0.5 * x
    return hx + hx * jnp.tanh(inner)


def _sigmoid(x):
    return 0.5 * jnp.tanh(0.5 * x) + 0.5


def _bdot(a, b):
    return jnp.dot(a, b, preferred_element_type=jnp.float32)


def _unpack_rows(w):
    return pltpu.bitcast(w, jnp.bfloat16)


def _mod_norm(x, gs, shift):
    r = jax.lax.rsqrt(jnp.mean(x * x, axis=-1, keepdims=True) + EPS)
    return (x * r) * gs + shift


def _block_time_index():
    rho = np.arange(SGU_BLOCK)
    return (rho % V7X_SUBLANES) * SEG + rho // V7X_SUBLANES


def _sublane_iota(cols):
    return jax.lax.broadcasted_iota(jnp.int32, (V7X_SUBLANES, cols), 0)


def _boundary_groups(blk_tail, prev_tail):
    sub = _sublane_iota(blk_tail.shape[1])
    groups = []
    for i in range(0, blk_tail.shape[0], V7X_SUBLANES):
        cur = pltpu.roll(blk_tail[i:i + V7X_SUBLANES], 1, 0)
        prv = pltpu.roll(prev_tail[i:i + V7X_SUBLANES], 1, 0)
        groups.append(jnp.where(sub == 0, prv, cur))
    return jnp.concatenate(groups, axis=0)


def _causal_conv_block(x_blk, prev_tail, taps, bias):
    halo = prev_tail.shape[0]
    n = x_blk.shape[0]
    ext = jnp.concatenate([_boundary_groups(x_blk[n - halo:], prev_tail), x_blk], axis=0)
    y = bias + ext[0:n] * taps[0]
    for k in range(1, len(taps)):
        y = y + ext[k * V7X_SUBLANES:k * V7X_SUBLANES + n] * taps[k]
    return y


def _sublane_scan(a, b, carry):
    sub = _sublane_iota(a.shape[1])
    b = b + jnp.where(sub == 0, a * carry, 0.0)
    k = 1
    while k < V7X_SUBLANES:
        am = jnp.where(sub < k, 0.0, a)
        b = b + am * pltpu.roll(b, k, 0)
        a = am * pltpu.roll(a, k, 0)
        k *= 2
    return b


def _linear_scan_block(a, u, carry):
    n_groups = a.shape[0] // V7X_SUBLANES
    grp = lambda v, g: v[g * V7X_SUBLANES:(g + 1) * V7X_SUBLANES]
    hs, ps = [grp(u, 0)], [grp(a, 0)]
    for g in range(1, n_groups):
        hs.append(grp(a, g) * hs[-1] + grp(u, g))
        ps.append(grp(a, g) * ps[-1])
    seg_end = _sublane_scan(ps[-1], hs[-1], carry)
    sub = _sublane_iota(a.shape[1])
    seg_in = jnp.where(sub == 0, carry, pltpu.roll(seg_end, 1, 0))
    h = jnp.concatenate([hh + pp * seg_in for hh, pp in zip(hs, ps)], axis=0)
    return h, seg_end[V7X_SUBLANES - 1:V7X_SUBLANES, :]


def _tile_row_copies(hbm_rows, vmem_tile, first_row, sem, to_vmem):
    copies = []
    first_seg = first_row // SEG
    for n in range(vmem_tile.shape[0] // SEG):
        for g in range(SEG):
            hbm = hbm_rows.at[pl.ds(first_seg + n * V7X_SUBLANES, V7X_SUBLANES), g, :]
            vmem = vmem_tile.at[n * SEG + g]
            src, dst = (hbm, vmem) if to_vmem else (vmem, hbm)
            copies.append(pltpu.make_async_copy(src, dst, sem))
    return copies


def _pack_kernel(w_ref, o_ref):
    o_ref[...] = pltpu.bitcast(w_ref[...].astype(jnp.bfloat16), jnp.uint32)


def _pack_rows(w, block_cols=PACK_COL_BLOCK, src_block=None):
    k, n = w.shape
    block_cols = min(block_cols, n)
    src = src_block or (lambda j: j)
    return pl.pallas_call(
        _pack_kernel,
        grid=(n // block_cols,),
        in_specs=[pl.BlockSpec((k, block_cols), lambda j: (0, src(j)))],
        out_specs=pl.BlockSpec((k // 2, block_cols), lambda j: (0, j)),
        out_shape=jax.ShapeDtypeStruct((k // 2, n), jnp.uint32),
        compiler_params=pltpu.CompilerParams(
            dimension_semantics=("arbitrary",), vmem_limit_bytes=VMEM_LIMIT_BYTES),
        name="pack_bf16",
    )(w)


def _adaln_kernel(ct_ref, w_ref, b_ref, o_ref):
    ct = ct_ref[...]
    act = ct * _sigmoid(ct)
    w = w_ref[...]
    rows = []
    for b in range(ct.shape[1]):
        rows.append(jnp.sum(act[:, b:b + 1] * w, axis=0, keepdims=True))
    o_ref[...] = jnp.concatenate(rows, axis=0) + b_ref[...]


def _adaln(c, w_ada, b_ada):
    bsz, d = c.shape
    n = w_ada.shape[1]
    bn = ADALN_COL_BLOCK
    return pl.pallas_call(
        _adaln_kernel,
        grid=(n // bn,),
        in_specs=[
            pl.BlockSpec((d, bsz), lambda j: (0, 0)),
            pl.BlockSpec((d, bn), lambda j: (0, j)),
            pl.BlockSpec((1, bn), lambda j: (0, j)),
        ],
        out_specs=pl.BlockSpec((bsz, bn), lambda j: (0, j)),
        out_shape=jax.ShapeDtypeStruct((bsz, n), jnp.float32),
        compiler_params=pltpu.CompilerParams(
            dimension_semantics=("arbitrary",), vmem_limit_bytes=VMEM_LIMIT_BYTES),
        name="adaln",
    )(c.T, w_ada, b_ada.reshape(1, n))


def _const_spec(shape):
    zeros = (0,) * len(shape)
    return pl.BlockSpec(shape, lambda g: zeros, pipeline_mode=pl.Buffered(1))


def _mixer_kernel(x_hbm, mod_ref, g_ref, w_in_ref, cw_ref, cb_ref, wax_ref, ba_ref, bx_ref,
                  lam_ref, lng_ref, lnb_ref, ws_ref, sb_ref, wba_ref, wbb_ref, wo_ref,
                  o_ref, xbuf, sems, z_buf, hb_buf, xr_tail, h_carry, merged_buf,
                  *, tiles_per_seq, n_tiles):
    t, d = o_ref.shape[1], o_ref.shape[2]
    hd = d // RNN_HEADS
    gd = d // SGU_GROUPS
    kw = cw_ref.shape[0]
    halo = (kw - 1) * V7X_SUBLANES
    n_blocks = t // SGU_BLOCK
    step = pl.program_id(0)
    tile = jnp.minimum(step, n_tiles - 1)
    slot = jax.lax.rem(tile, 3)
    slot_prev = jax.lax.rem(step + 2, 3)
    seq_start = jax.lax.rem(tile, tiles_per_seq) == 0

    def x_copies(tile_idx, slot_idx):
        return _tile_row_copies(x_hbm, xbuf.at[slot_idx], tile_idx * t, sems.at[slot_idx], True)

    @pl.when(step == 0)
    def _():
        for cp in x_copies(0, 0):
            cp.start()
        merged_buf[...] = jnp.zeros_like(merged_buf)
        xbuf[2] = jnp.zeros(xbuf.shape[1:], jnp.float32)

    @pl.when(step < n_tiles)
    def _():
        for cp in x_copies(step, slot):
            cp.wait()

    @pl.when(step + 1 < n_tiles)
    def _():
        for cp in x_copies(step + 1, jax.lax.rem(step + 1, 3)):
            cp.start()

    gate1_prev = mod_ref[jnp.maximum(step - 1, 0) // tiles_per_seq][:, 2 * d:3 * d]
    mo = _bdot(merged_buf[...], _unpack_rows(wo_ref[...]))
    o_ref[0] = xbuf[slot_prev].reshape(t, d) + gate1_prev * mo

    mod = mod_ref[tile // tiles_per_seq]
    shift1, scale1 = mod[:, 0:d], mod[:, d:2 * d]
    x = xbuf[slot].reshape(t, d)
    hb_buf[...] = _mod_norm(x, g_ref[...] * (1.0 + scale1), shift1).astype(jnp.bfloat16)

    def in_proj(j):
        z_buf[j] = _bdot(hb_buf[...], _unpack_rows(w_in_ref[:, j * d:(j + 1) * d]))

    in_proj(0)
    in_proj(3)
    in_proj(2)

    taps = [cw_ref[k:k + 1, :] for k in range(kw)]
    prev_tail = jnp.where(seq_start, 0.0, xr_tail[...])
    xc_blocks = []
    for n in range(n_blocks):
        blk = z_buf[0, n * SGU_BLOCK:(n + 1) * SGU_BLOCK, :]
        xc_blocks.append(_causal_conv_block(blk, prev_tail, taps, cb_ref[...]))
        prev_tail = blk[SGU_BLOCK - halo:]
    xr_tail[...] = prev_tail
    xc = jnp.concatenate(xc_blocks, axis=0)
    xcb = xc.astype(jnp.bfloat16)
    pre = [_bdot(xcb[:, i * hd:(i + 1) * hd], _unpack_rows(wax_ref[i])) for i in range(RNN_HEADS)]

    in_proj(1)
    in_proj(4)
    in_proj(5)

    r = _sigmoid(jnp.concatenate([p[:, :hd] for p in pre], axis=1) + ba_ref[...])
    gi = _sigmoid(jnp.concatenate([p[:, hd:] for p in pre], axis=1) + bx_ref[...])
    lam = lam_ref[...]
    log_sig = jnp.minimum(lam, 0.0) - jnp.log1p(jnp.exp(-jnp.abs(lam)))
    log_a = r * (LRU_C * log_sig)
    a = jnp.exp(log_a)
    mult = jnp.sqrt(-jnp.tanh(log_a) * (a * a + 1.0))
    u = mult * (gi * xc)
    state = jnp.where(seq_start, 0.0, h_carry[...])
    hs_blocks = []
    for n in range(n_blocks):
        rows = slice(n * SGU_BLOCK, (n + 1) * SGU_BLOCK)
        hs, state = _linear_scan_block(a[rows], u[rows], state)
        hs_blocks.append(hs)
    h_carry[...] = state
    ya_pre = (jnp.concatenate(hs_blocks, axis=0) * _gelu(z_buf[1])).astype(jnp.bfloat16)

    gu = _gelu(z_buf[2])
    gv = _gelu(z_buf[3])
    mu = jnp.mean(gv, axis=-1, keepdims=True)
    gc = gv - mu
    var = jnp.mean(gc * gc, axis=-1, keepdims=True)
    vb = ((gc * jax.lax.rsqrt(var + EPS)) * lng_ref[...] + lnb_ref[...]).astype(jnp.bfloat16)
    ri = jax.lax.broadcasted_iota(jnp.int32, (SGU_BLOCK, SGU_BLOCK), 0)
    ci = jax.lax.broadcasted_iota(jnp.int32, (SGU_BLOCK, SGU_BLOCK), 1)
    time_of = lambda i: (i & (V7X_SUBLANES - 1)) * SEG + (i >> 3)
    causal = time_of(ci) <= time_of(ri)
    wm = [jnp.where(causal, ws_ref[g], 0.0).astype(jnp.bfloat16) for g in range(SGU_GROUPS)]
    blocks = []
    for n in range(n_blocks):
        rows = slice(n * SGU_BLOCK, (n + 1) * SGU_BLOCK)
        mixed = jnp.concatenate(
            [_bdot(wm[g], vb[rows, g * gd:(g + 1) * gd]) for g in range(SGU_GROUPS)], axis=1)
        blocks.append(mixed + sb_ref[...])
    yb_pre = (gu * jnp.concatenate(blocks, axis=0)).astype(jnp.bfloat16)

    ya = _bdot(ya_pre, _unpack_rows(wba_ref[...]))
    yb = _bdot(yb_pre, _unpack_rows(wbb_ref[...]))
    merged = _sigmoid(z_buf[4]) * ya + _sigmoid(z_buf[5]) * yb
    merged_buf[...] = merged.astype(jnp.bfloat16)


def _mixer(x, mod3, norm_g, w_in, conv_w, conv_b, wax, b_a, b_x, lam, ln_g, ln_b, w_s, sgu_bias,
           w_ba, w_bb, w_out):
    bsz, seq, d = x.shape
    t = TIME_TILE
    n_t = seq // t
    n_tiles = bsz * n_t
    consts = [mod3, norm_g, w_in, conv_w, conv_b, wax, b_a, b_x, lam, ln_g, ln_b, w_s, sgu_bias,
              w_ba, w_bb, w_out]
    halo = (conv_w.shape[0] - 1) * V7X_SUBLANES
    return pl.pallas_call(
        functools.partial(_mixer_kernel, tiles_per_seq=n_t, n_tiles=n_tiles),
        grid=(n_tiles + 1,),
        in_specs=[pl.BlockSpec(memory_space=pl.ANY)] + [_const_spec(a.shape) for a in consts],
        out_specs=pl.BlockSpec((1, t, d), lambda g: (jnp.maximum(g - 1, 0), 0, 0)),
        out_shape=jax.ShapeDtypeStruct((n_tiles, t, d), jnp.float32),
        scratch_shapes=[
            pltpu.VMEM((3, t // V7X_SUBLANES, V7X_SUBLANES, d), jnp.float32),
            pltpu.SemaphoreType.DMA((3,)),
            pltpu.VMEM((6, t, d), jnp.float32),
            pltpu.VMEM((t, d), jnp.bfloat16),
            pltpu.VMEM((halo, d), jnp.float32),
            pltpu.VMEM((1, d), jnp.float32),
            pltpu.VMEM((t, d), jnp.bfloat16),
        ],
        compiler_params=pltpu.CompilerParams(
            dimension_semantics=("arbitrary",), vmem_limit_bytes=VMEM_LIMIT_BYTES),
        name="mixer",
    )(x.reshape(bsz * seq // SEG, SEG, d), *consts)


def _ffn_kernel(xcur_ref, xnext_ref, mod_ref, g_ref, wup_ref, cw_ref, cb_ref,
                wdn_ref, gf_ref, out_hbm, up_buf, tail_buf, hb_cur, hb_next, acc_buf, p_buf,
                obuf, sems, *, final_norm, tiles_per_seq, n_tiles, n_chunks):
    t = xcur_ref.shape[1]
    d = xcur_ref.shape[2]
    cw = wdn_ref.shape[0] * 2 // n_chunks
    kw = cw_ref.shape[0]
    halo = (kw - 1) * V7X_SUBLANES
    n_blocks = t // SGU_BLOCK
    step = pl.program_id(0)
    seq_start = jax.lax.rem(step, tiles_per_seq) == 0
    g_norm = g_ref[...]
    oslot = jax.lax.rem(step, 2)

    def out_copies(tile_idx, slot_idx):
        return _tile_row_copies(out_hbm, obuf.at[slot_idx], tile_idx * t, sems.at[slot_idx], False)

    def norm_rows(x, b):
        mod = mod_ref[b]
        return _mod_norm(x, g_norm * (1.0 + mod[:, 4 * d:5 * d]),
                         mod[:, 3 * d:4 * d]).astype(jnp.bfloat16)

    def up_proj(k, hb_ref):
        cols = slice(k * 2 * cw, (k + 1) * 2 * cw)
        up_buf[k] = _bdot(hb_ref[...], _unpack_rows(wup_ref[:, cols]))

    def down_group(g):
        lo, hi = g * FFN_DOWN_GROUP * cw, (g + 1) * FFN_DOWN_GROUP * cw
        part = _bdot(p_buf[:, lo:hi], _unpack_rows(wdn_ref[lo // 2:hi // 2, :]))
        if g == 0:
            acc_buf[...] = part
        else:
            acc_buf[...] += part

    @pl.when(step == 0)
    def _():
        hb_next[...] = norm_rows(xcur_ref[0], 0)
        for k in range(FFN_HEAD_CHUNKS):
            up_proj(k, hb_next)

    @pl.when(step >= 2)
    def _():
        for cp in out_copies(step - 2, oslot):
            cp.wait()

    hb_cur[...] = hb_next[...]

    b_next = jnp.minimum(step + 1, n_tiles - 1) // tiles_per_seq
    for r in range(0, t, FFN_ROW_PIECE):
        rows = slice(r, r + FFN_ROW_PIECE)
        hb_next[rows, :] = norm_rows(xnext_ref[0, rows, :], b_next)

    for k in range(n_chunks):
        cols = slice(k * 2 * cw, (k + 1) * 2 * cw)
        if k >= FFN_HEAD_CHUNKS:
            up_proj(k, hb_cur)
        taps = [cw_ref[j:j + 1, cols] for j in range(kw)]
        prev_tail = jnp.where(seq_start, 0.0, tail_buf[k])
        y_blocks = []
        for n in range(n_blocks):
            blk = up_buf[k, n * SGU_BLOCK:(n + 1) * SGU_BLOCK, :]
            y_blocks.append(_causal_conv_block(blk, prev_tail, taps, cb_ref[:, cols]))
            prev_tail = blk[SGU_BLOCK - halo:]
        tail_buf[k] = prev_tail
        y = jnp.concatenate(y_blocks, axis=0)
        p_buf[:, k * cw:(k + 1) * cw] = (_gelu(y[:, :cw]) * y[:, cw:]).astype(jnp.bfloat16)
        for g, after in enumerate(FFN_DOWN_AFTER):
            if after == k:
                down_group(g)

    for k in range(FFN_HEAD_CHUNKS):
        up_proj(k, hb_next)

    gate2 = mod_ref[step // tiles_per_seq][:, 5 * d:6 * d]
    for r in range(0, t, FFN_ROW_PIECE):
        rows = slice(r, r + FFN_ROW_PIECE)
        y = xcur_ref[0, rows, :] + gate2 * acc_buf[rows, :]
        if final_norm:
            y = (y * jax.lax.rsqrt(jnp.mean(y * y, axis=-1, keepdims=True) + EPS)) * gf_ref[...]
        groups = slice(r // V7X_SUBLANES, (r + FFN_ROW_PIECE) // V7X_SUBLANES)
        obuf[oslot, groups] = y.reshape(FFN_ROW_PIECE // V7X_SUBLANES, V7X_SUBLANES, d)

    for cp in out_copies(step, oslot):
        cp.start()

    @pl.when(step == n_tiles - 1)
    def _():
        for cp in out_copies(step - 1, 1 - oslot):
            cp.wait()
        for cp in out_copies(step, oslot):
            cp.wait()


def _ffn(x_tiles, mod3, norm_g, w_up, conv_w, conv_b, w_down, norm_final_g, final_norm, out_shape):
    n_tiles, t, d = x_tiles.shape
    bsz, seq, _ = out_shape
    dff = w_down.shape[0]
    nc = FFN_CHUNKS
    cw = dff // nc
    n_t = seq // t
    halo = (conv_w.shape[0] - 1) * V7X_SUBLANES
    perm = lambda a: a.reshape(-1, 2, nc, cw).transpose(0, 2, 1, 3).reshape(-1, 2 * dff)
    wup = _pack_rows(w_up, block_cols=cw, src_block=lambda j: (j % 2) * nc + j // 2)
    consts = [mod3, norm_g, wup, perm(conv_w), perm(conv_b), _pack_rows(w_down), norm_final_g]
    out = pl.pallas_call(
        functools.partial(_ffn_kernel, final_norm=final_norm, tiles_per_seq=n_t,
                          n_tiles=n_tiles, n_chunks=nc),
        grid=(n_tiles,),
        in_specs=[
            pl.BlockSpec((1, t, d), lambda g: (g, 0, 0)),
            pl.BlockSpec((1, t, d), lambda g: (jnp.minimum(g + 1, n_tiles - 1), 0, 0)),
        ] + [_const_spec(a.shape) for a in consts],
        out_specs=pl.BlockSpec(memory_space=pl.ANY),
        out_shape=jax.ShapeDtypeStruct((bsz * seq // SEG, SEG, d), jnp.float32),
        scratch_shapes=[
            pltpu.VMEM((nc, t, 2 * cw), jnp.float32),
            pltpu.VMEM((nc, halo, 2 * cw), jnp.float32),
            pltpu.VMEM((t, d), jnp.bfloat16),
            pltpu.VMEM((t, d), jnp.bfloat16),
            pltpu.VMEM((t, d), jnp.float32),
            pltpu.VMEM((t, dff), jnp.bfloat16),
            pltpu.VMEM((2, t // V7X_SUBLANES, V7X_SUBLANES, d), jnp.float32),
            pltpu.SemaphoreType.DMA((2,)),
        ],
        compiler_params=pltpu.CompilerParams(
            dimension_semantics=("arbitrary",), vmem_limit_bytes=VMEM_LIMIT_BYTES),
        name="ffn",
    )(x_tiles, x_tiles, *consts)
    return out.reshape(bsz, seq, d)


def kernel(x, c, w_ada, b_ada, norm_mix_g, w_in, rnn_conv_w, rnn_conv_b, lru_w_a, lru_b_a, lru_w_x, lru_b_x, lru_lambda, sgu_ln_g, sgu_ln_b, sgu_w_s, sgu_b_s, w_branch_a, w_branch_b, w_out, norm_ffn_g, w_up, ffn_conv_w, ffn_conv_b, w_down, norm_final_g):
    depth = w_in.shape[0]
    d = x.shape[-1]
    hd = d // RNN_HEADS
    row = lambda v: v.reshape(1, -1)
    time_of = _block_time_index()
    for l in range(depth):
        mod3 = _adaln(c, w_ada[l], b_ada[l]).reshape(x.shape[0], 1, N_MOD * d)
        wax = jnp.concatenate([lru_w_a[l], lru_w_x[l]], axis=-1)
        wax = _pack_rows(wax.reshape(d, 2 * hd)).reshape(RNN_HEADS, hd // 2, 2 * hd)
        w_s = sgu_w_s[l][:, time_of, :][:, :, time_of]
        sgu_bias = jnp.repeat(sgu_b_s[l].T, d // SGU_GROUPS, axis=1)[time_of, :]
        x1 = _mixer(x, mod3, row(norm_mix_g[l]), _pack_rows(w_in[l]), rnn_conv_w[l],
                    row(rnn_conv_b[l]), wax, row(lru_b_a[l]), row(lru_b_x[l]), row(lru_lambda[l]),
                    row(sgu_ln_g[l]), row(sgu_ln_b[l]), w_s, sgu_bias,
                    _pack_rows(w_branch_a[l]), _pack_rows(w_branch_b[l]), _pack_rows(w_out[l]))
        x = _ffn(x1, mod3, row(norm_ffn_g[l]), w_up[l], ffn_conv_w[l],
                 ffn_conv_b[l], w_down[l], row(norm_final_g),
                 final_norm=(l == depth - 1), out_shape=x.shape)
    return x
```

```python
import functools

import jax
import jax.numpy as jnp
import numpy as np
from jax.experimental import pallas as pl
from jax.experimental.pallas import tpu as pltpu

EPS = 1e-6
LRU_C = 8.0
RNN_HEADS = 8
SGU_GROUPS = 8
SGU_BLOCK = 128
N_MOD = 6

V7X_SUBLANES = 8
TIME_TILE = 256
SEG = SGU_BLOCK // V7X_SUBLANES
FFN_CHUNKS = 12
FFN_ROW_PIECE = 64
FFN_DOWN_GROUP = 3
FFN_DOWN_AFTER = (4, 7, 11, 11)
ADALN_COL_BLOCK = 1536
PACK_COL_BLOCK = 512
PACK_WIDE_COL_BLOCK = 1536
PACK_PAIRED_COL_BLOCK = 768
VMEM_LIMIT_BYTES = 56 * 1024 * 1024

_GELU_C0 = 0.7978845608028654
_GELU_C1 = 0.044715


def _gelu(x):
    inner = x * (_GELU_C0 + (_GELU_C0 * _GELU_C1) * (x * x))
    hx = 0.5 * x
    return hx + hx * jnp.tanh(inner)


def _sigmoid(x):
    return 0.5 * jnp.tanh(0.5 * x) + 0.5


def _bdot(a, b):
    return jnp.dot(a, b, preferred_element_type=jnp.float32)


def _unpack_rows(w):
    return pltpu.bitcast(w, jnp.bfloat16)


def _mod_norm(x, gs, shift):
    r = jax.lax.rsqrt(jnp.mean(x * x, axis=-1, keepdims=True) + EPS)
    return (x * r) * gs + shift


def _block_time_index():
    rho = np.arange(SGU_BLOCK)
    return (rho % V7X_SUBLANES) * SEG + rho // V7X_SUBLANES


def _sublane_iota(cols):
    return jax.lax.broadcasted_iota(jnp.int32, (V7X_SUBLANES, cols), 0)


def _boundary_groups(blk_tail, prev_tail):
    sub = _sublane_iota(blk_tail.shape[1])
    groups = []
    for i in range(0, blk_tail.shape[0], V7X_SUBLANES):
        cur = pltpu.roll(blk_tail[i:i + V7X_SUBLANES], 1, 0)
        prv = pltpu.roll(prev_tail[i:i + V7X_SUBLANES], 1, 0)
        groups.append(jnp.where(sub == 0, prv, cur))
    return jnp.concatenate(groups, axis=0)


def _causal_conv_block(x_blk, prev_tail, taps, bias):
    halo = prev_tail.shape[0]
    n = x_blk.shape[0]
    ext = jnp.concatenate([_boundary_groups(x_blk[n - halo:], prev_tail), x_blk], axis=0)
    y = bias + ext[0:n] * taps[0]
    for k in range(1, len(taps)):
        y = y + ext[k * V7X_SUBLANES:k * V7X_SUBLANES + n] * taps[k]
    return y


def _sublane_scan(a, b, carry):
    sub = _sublane_iota(a.shape[1])
    b = b + jnp.where(sub == 0, a * carry, 0.0)
    k = 1
    while k < V7X_SUBLANES:
        am = jnp.where(sub < k, 0.0, a)
        b = b + am * pltpu.roll(b, k, 0)
        a = am * pltpu.roll(a, k, 0)
        k *= 2
    return b


def _linear_scan_block(a, u, carry):
    n_groups = a.shape[0] // V7X_SUBLANES
    grp = lambda v, g: v[g * V7X_SUBLANES:(g + 1) * V7X_SUBLANES]
    hs, ps = [grp(u, 0)], [grp(a, 0)]
    for g in range(1, n_groups):
        hs.append(grp(a, g) * hs[-1] + grp(u, g))
        ps.append(grp(a, g) * ps[-1])
    seg_end = _sublane_scan(ps[-1], hs[-1], carry)
    sub = _sublane_iota(a.shape[1])
    seg_in = jnp.where(sub == 0, carry, pltpu.roll(seg_end, 1, 0))
    h = jnp.concatenate([hh + pp * seg_in for hh, pp in zip(hs, ps)], axis=0)
    return h, seg_end[V7X_SUBLANES - 1:V7X_SUBLANES, :]


def _tile_row_copies(hbm_rows, vmem_tile, first_row, sem, to_vmem):
    copies = []
    first_seg = first_row // SEG
    for n in range(vmem_tile.shape[0] // SEG):
        for g in range(SEG):
            hbm = hbm_rows.at[pl.ds(first_seg + n * V7X_SUBLANES, V7X_SUBLANES), g, :]
            vmem = vmem_tile.at[n * SEG + g]
            src, dst = (hbm, vmem) if to_vmem else (vmem, hbm)
            copies.append(pltpu.make_async_copy(src, dst, sem))
    return copies


def _pack_bits(w):
    return pltpu.bitcast(w.astype(jnp.bfloat16), jnp.uint32)


def _pack_kernel(*refs):
    n = len(refs) // 2
    for w_ref, o_ref in zip(refs[:n], refs[n:]):
        o_ref[...] = _pack_bits(w_ref[...])


def _pack_rows(ws, block_cols):
    k, n = ws[0].shape
    return pl.pallas_call(
        _pack_kernel,
        grid=(n // block_cols,),
        in_specs=[pl.BlockSpec((k, block_cols), lambda j: (0, j))] * len(ws),
        out_specs=[pl.BlockSpec((k // 2, block_cols), lambda j: (0, j))] * len(ws),
        out_shape=[jax.ShapeDtypeStruct((k // 2, n), jnp.uint32)] * len(ws),
        compiler_params=pltpu.CompilerParams(
            dimension_semantics=("arbitrary",), vmem_limit_bytes=VMEM_LIMIT_BYTES),
        name="pack_bf16",
    )(*ws)


def _pack_paired_kernel(a_ref, b_ref, o_ref, *, chunk):
    a, b = _pack_bits(a_ref[...]), _pack_bits(b_ref[...])
    parts = []
    for c in range(0, a.shape[1], chunk):
        parts += [a[:, c:c + chunk], b[:, c:c + chunk]]
    o_ref[...] = jnp.concatenate(parts, axis=1)


def _pack_rows_paired(w, chunk, block_cols):
    k, n = w.shape
    half_blocks = n // 2 // block_cols
    return pl.pallas_call(
        functools.partial(_pack_paired_kernel, chunk=chunk),
        grid=(half_blocks,),
        in_specs=[pl.BlockSpec((k, block_cols), lambda j: (0, j)),
                  pl.BlockSpec((k, block_cols), lambda j: (0, half_blocks + j))],
        out_specs=pl.BlockSpec((k // 2, 2 * block_cols), lambda j: (0, j)),
        out_shape=jax.ShapeDtypeStruct((k // 2, n), jnp.uint32),
        compiler_params=pltpu.CompilerParams(
            dimension_semantics=("arbitrary",), vmem_limit_bytes=VMEM_LIMIT_BYTES),
        name="pack_bf16_paired",
    )(w, w)


def _adaln_kernel(ct_ref, w_ref, b_ref, o_ref):
    ct = ct_ref[...]
    act = ct * _sigmoid(ct)
    w = w_ref[...]
    rows = []
    for b in range(ct.shape[1]):
        rows.append(jnp.sum(act[:, b:b + 1] * w, axis=0, keepdims=True))
    o_ref[...] = jnp.concatenate(rows, axis=0) + b_ref[...]


def _adaln(c, w_ada, b_ada):
    bsz, d = c.shape
    n = w_ada.shape[1]
    bn = ADALN_COL_BLOCK
    return pl.pallas_call(
        _adaln_kernel,
        grid=(n // bn,),
        in_specs=[
            pl.BlockSpec((d, bsz), lambda j: (0, 0)),
            pl.BlockSpec((d, bn), lambda j: (0, j)),
            pl.BlockSpec((1, bn), lambda j: (0, j)),
        ],
        out_specs=pl.BlockSpec((bsz, bn), lambda j: (0, j)),
        out_shape=jax.ShapeDtypeStruct((bsz, n), jnp.float32),
        compiler_params=pltpu.CompilerParams(
            dimension_semantics=("arbitrary",), vmem_limit_bytes=VMEM_LIMIT_BYTES),
        name="adaln",
    )(c.T, w_ada, b_ada.reshape(1, n))


def _const_spec(shape):
    zeros = (0,) * len(shape)
    return pl.BlockSpec(shape, lambda g: zeros, pipeline_mode=pl.Buffered(1))


def _mixer_kernel(x_hbm, mod_ref, g_ref, w_in_ref, cw_ref, cb_ref, wax_ref, ba_ref, bx_ref,
                  lam_ref, lng_ref, lnb_ref, ws_ref, sb_ref, wba_ref, wbb_ref, wo_ref,
                  o_ref, xbuf, sems, z_buf, hb_buf, xr_tail, h_carry, merged_buf,
                  *, tiles_per_seq, n_tiles):
    t, d = o_ref.shape[1], o_ref.shape[2]
    hd = d // RNN_HEADS
    gd = d // SGU_GROUPS
    kw = cw_ref.shape[0]
    halo = (kw - 1) * V7X_SUBLANES
    n_blocks = t // SGU_BLOCK
    step = pl.program_id(0)
    tile = jnp.minimum(step, n_tiles - 1)
    slot = jax.lax.rem(tile, 3)
    slot_prev = jax.lax.rem(step + 2, 3)
    seq_start = jax.lax.rem(tile, tiles_per_seq) == 0

    def x_copies(tile_idx, slot_idx):
        return _tile_row_copies(x_hbm, xbuf.at[slot_idx], tile_idx * t, sems.at[slot_idx], True)

    @pl.when(step == 0)
    def _():
        for cp in x_copies(0, 0):
            cp.start()
        merged_buf[...] = jnp.zeros_like(merged_buf)
        xbuf[2] = jnp.zeros(xbuf.shape[1:], jnp.float32)

    @pl.when(step < n_tiles)
    def _():
        for cp in x_copies(step, slot):
            cp.wait()

    @pl.when(step + 1 < n_tiles)
    def _():
        for cp in x_copies(step + 1, jax.lax.rem(step + 1, 3)):
            cp.start()

    gate1_prev = mod_ref[jnp.maximum(step - 1, 0) // tiles_per_seq][:, 2 * d:3 * d]
    mo = _bdot(merged_buf[...], _unpack_rows(wo_ref[...]))
    o_ref[0] = xbuf[slot_prev].reshape(t, d) + gate1_prev * mo

    mod = mod_ref[tile // tiles_per_seq]
    shift1, scale1 = mod[:, 0:d], mod[:, d:2 * d]
    x = xbuf[slot].reshape(t, d)
    hb_buf[...] = _mod_norm(x, g_ref[...] * (1.0 + scale1), shift1).astype(jnp.bfloat16)

    def in_proj(j):
        z_buf[j] = _bdot(hb_buf[...], _unpack_rows(w_in_ref[:, j * d:(j + 1) * d]))

    in_proj(0)
    in_proj(3)
    in_proj(2)

    taps = [cw_ref[k:k + 1, :] for k in range(kw)]
    prev_tail = jnp.where(seq_start, 0.0, xr_tail[...])
    xc_blocks = []
    for n in range(n_blocks):
        blk = z_buf[0, n * SGU_BLOCK:(n + 1) * SGU_BLOCK, :]
        xc_blocks.append(_causal_conv_block(blk, prev_tail, taps, cb_ref[...]))
        prev_tail = blk[SGU_BLOCK - halo:]
    xr_tail[...] = prev_tail
    xc = jnp.concatenate(xc_blocks, axis=0)
    xcb = xc.astype(jnp.bfloat16)
    pre = [_bdot(xcb[:, i * 2 * hd:(i + 1) * 2 * hd], _unpack_rows(wax_ref[i]))
           for i in range(RNN_HEADS // 2)]

    in_proj(1)
    in_proj(4)
    in_proj(5)

    r = _sigmoid(jnp.concatenate([p[:, :2 * hd] for p in pre], axis=1) + ba_ref[...])
    gi = _sigmoid(jnp.concatenate([p[:, 2 * hd:] for p in pre], axis=1) + bx_ref[...])
    lam = lam_ref[...]
    log_sig = jnp.minimum(lam, 0.0) - jnp.log1p(jnp.exp(-jnp.abs(lam)))
    log_a = r * (LRU_C * log_sig)
    a = jnp.exp(log_a)
    mult = jnp.sqrt(-jnp.tanh(log_a) * (a * a + 1.0))
    u = mult * (gi * xc)
    state = jnp.where(seq_start, 0.0, h_carry[...])
    hs_blocks = []
    for n in range(n_blocks):
        rows = slice(n * SGU_BLOCK, (n + 1) * SGU_BLOCK)
        hs, state = _linear_scan_block(a[rows], u[rows], state)
        hs_blocks.append(hs)
    h_carry[...] = state
    ya_pre = (jnp.concatenate(hs_blocks, axis=0) * _gelu(z_buf[1])).astype(jnp.bfloat16)

    gu = _gelu(z_buf[2])
    gv = _gelu(z_buf[3])
    mu = jnp.mean(gv, axis=-1, keepdims=True)
    gc = gv - mu
    var = jnp.mean(gc * gc, axis=-1, keepdims=True)
    vb = ((gc * jax.lax.rsqrt(var + EPS)) * lng_ref[...] + lnb_ref[...]).astype(jnp.bfloat16)
    ri = jax.lax.broadcasted_iota(jnp.int32, (SGU_BLOCK, SGU_BLOCK), 0)
    ci = jax.lax.broadcasted_iota(jnp.int32, (SGU_BLOCK, SGU_BLOCK), 1)
    time_of = lambda i: (i & (V7X_SUBLANES - 1)) * SEG + (i >> 3)
    causal = time_of(ci) <= time_of(ri)
    wm = [jnp.where(causal, ws_ref[g], 0.0).astype(jnp.bfloat16) for g in range(SGU_GROUPS)]
    mixed = []
    for g in range(SGU_GROUPS):
        v_g = jnp.concatenate([vb[n * SGU_BLOCK:(n + 1) * SGU_BLOCK, g * gd:(g + 1) * gd]
                               for n in range(n_blocks)], axis=1)
        mixed.append(_bdot(wm[g], v_g))
    blocks = [jnp.concatenate([m[:, n * gd:(n + 1) * gd] for m in mixed], axis=1) + sb_ref[...]
              for n in range(n_blocks)]
    yb_pre = (gu * jnp.concatenate(blocks, axis=0)).astype(jnp.bfloat16)

    ya = _bdot(ya_pre, _unpack_rows(wba_ref[...]))
    yb = _bdot(yb_pre, _unpack_rows(wbb_ref[...]))
    merged = _sigmoid(z_buf[4]) * ya + _sigmoid(z_buf[5]) * yb
    merged_buf[...] = merged.astype(jnp.bfloat16)


def _mixer(x, mod3, norm_g, w_in, conv_w, conv_b, wax, b_a, b_x, lam, ln_g, ln_b, w_s, sgu_bias,
           w_ba, w_bb, w_out):
    bsz, seq, d = x.shape
    t = TIME_TILE
    n_t = seq // t
    n_tiles = bsz * n_t
    consts = [mod3, norm_g, w_in, conv_w, conv_b, wax, b_a, b_x, lam, ln_g, ln_b, w_s, sgu_bias,
              w_ba, w_bb, w_out]
    halo = (conv_w.shape[0] - 1) * V7X_SUBLANES
    return pl.pallas_call(
        functools.partial(_mixer_kernel, tiles_per_seq=n_t, n_tiles=n_tiles),
        grid=(n_tiles + 1,),
        in_specs=[pl.BlockSpec(memory_space=pl.ANY)] + [_const_spec(a.shape) for a in consts],
        out_specs=pl.BlockSpec((1, t, d), lambda g: (jnp.maximum(g - 1, 0), 0, 0)),
        out_shape=jax.ShapeDtypeStruct((n_tiles, t, d), jnp.float32),
        scratch_shapes=[
            pltpu.VMEM((3, t // V7X_SUBLANES, V7X_SUBLANES, d), jnp.float32),
            pltpu.SemaphoreType.DMA((3,)),
            pltpu.VMEM((6, t, d), jnp.float32),
            pltpu.VMEM((t, d), jnp.bfloat16),
            pltpu.VMEM((halo, d), jnp.float32),
            pltpu.VMEM((1, d), jnp.float32),
            pltpu.VMEM((t, d), jnp.bfloat16),
        ],
        compiler_params=pltpu.CompilerParams(
            dimension_semantics=("arbitrary",), vmem_limit_bytes=VMEM_LIMIT_BYTES),
        name="mixer",
    )(x.reshape(bsz * seq // SEG, SEG, d), *consts)


def _ffn_kernel(x0_ref, xnext_ref, xres_ref, mod_ref, g_ref, wup_ref, cw_ref, cb_ref,
                wdn_ref, gf_ref, out_hbm, up_buf, tail_buf, hb_cur, hb_next, acc_buf, p_buf,
                obuf, sems, *, final_norm, tiles_per_seq, n_tiles, n_chunks):
    t = xnext_ref.shape[1]
    d = xnext_ref.shape[2]
    cw = wdn_ref.shape[0] * 2 // n_chunks
    kw = cw_ref.shape[0]
    halo = (kw - 1) * V7X_SUBLANES
    n_blocks = t // SGU_BLOCK
    step = pl.program_id(0)
    seq_start = jax.lax.rem(step, tiles_per_seq) == 0
    g_norm = g_ref[...]
    oslot = jax.lax.rem(step + 1, 2)

    def out_copies(tile_idx, slot_idx):
        return _tile_row_copies(out_hbm, obuf.at[slot_idx], tile_idx * t, sems.at[slot_idx], False)

    def norm_rows(x, b):
        mod = mod_ref[b]
        return _mod_norm(x, g_norm * (1.0 + mod[:, 4 * d:5 * d]),
                         mod[:, 3 * d:4 * d]).astype(jnp.bfloat16)

    @pl.when(step == 0)
    def _():
        hb_next[...] = norm_rows(x0_ref[0], 0)
        acc_buf[...] = jnp.zeros_like(acc_buf)

    @pl.when(step >= 3)
    def _():
        for cp in out_copies(step - 3, oslot):
            cp.wait()

    hb_cur[...] = hb_next[...]

    b_res = jnp.maximum(step - 1, 0) // tiles_per_seq
    b_next = jnp.minimum(step + 1, n_tiles - 1) // tiles_per_seq
    gate2 = mod_ref[b_res][:, 5 * d:6 * d]
    for r in range(0, t, FFN_ROW_PIECE):
        rows = slice(r, r + FFN_ROW_PIECE)
        y = xres_ref[0, rows, :] + gate2 * acc_buf[rows, :]
        if final_norm:
            y = (y * jax.lax.rsqrt(jnp.mean(y * y, axis=-1, keepdims=True) + EPS)) * gf_ref[...]
        groups = slice(r // V7X_SUBLANES, (r + FFN_ROW_PIECE) // V7X_SUBLANES)
        obuf[oslot, groups] = y.reshape(FFN_ROW_PIECE // V7X_SUBLANES, V7X_SUBLANES, d)
        hb_next[rows, :] = norm_rows(xnext_ref[0, rows, :], b_next)

    def down_group(g):
        lo, hi = g * FFN_DOWN_GROUP * cw, (g + 1) * FFN_DOWN_GROUP * cw
        part = _bdot(p_buf[:, lo:hi], _unpack_rows(wdn_ref[lo // 2:hi // 2, :]))
        if g == 0:
            acc_buf[...] = part
        else:
            acc_buf[...] += part

    for k in range(n_chunks):
        cols = slice(k * 2 * cw, (k + 1) * 2 * cw)
        up_buf[k] = _bdot(hb_cur[...], _unpack_rows(wup_ref[:, cols]))
        taps = [cw_ref[j:j + 1, cols] for j in range(kw)]
        prev_tail = jnp.where(seq_start, 0.0, tail_buf[k])
        y_blocks = []
        for n in range(n_blocks):
            blk = up_buf[k, n * SGU_BLOCK:(n + 1) * SGU_BLOCK, :]
            y_blocks.append(_causal_conv_block(blk, prev_tail, taps, cb_ref[:, cols]))
            prev_tail = blk[SGU_BLOCK - halo:]
        tail_buf[k] = prev_tail
        y = jnp.concatenate(y_blocks, axis=0)
        p_buf[:, k * cw:(k + 1) * cw] = (_gelu(y[:, :cw]) * y[:, cw:]).astype(jnp.bfloat16)
        for g, after in enumerate(FFN_DOWN_AFTER):
            if after == k:
                down_group(g)

    @pl.when(step >= 1)
    def _():
        for cp in out_copies(step - 1, oslot):
            cp.start()

    @pl.when(step == n_tiles)
    def _():
        for cp in out_copies(step - 2, 1 - oslot):
            cp.wait()
        for cp in out_copies(step - 1, oslot):
            cp.wait()


def _ffn(x_tiles, mod3, norm_g, w_up, conv_w, conv_b, w_down, norm_final_g, final_norm, out_shape):
    n_tiles, t, d = x_tiles.shape
    bsz, seq, _ = out_shape
    dff = w_down.shape[0]
    nc = FFN_CHUNKS
    cw = dff // nc
    n_t = seq // t
    halo = (conv_w.shape[0] - 1) * V7X_SUBLANES
    perm = lambda a: a.reshape(-1, 2, nc, cw).transpose(0, 2, 1, 3).reshape(-1, 2 * dff)
    wup = _pack_rows_paired(w_up, chunk=cw, block_cols=PACK_PAIRED_COL_BLOCK)
    (wdn,) = _pack_rows([w_down], PACK_COL_BLOCK)
    consts = [mod3, norm_g, wup, perm(conv_w), perm(conv_b), wdn, norm_final_g]
    tile_res = lambda g: jnp.maximum(g - 1, 0)
    out = pl.pallas_call(
        functools.partial(_ffn_kernel, final_norm=final_norm, tiles_per_seq=n_t,
                          n_tiles=n_tiles, n_chunks=nc),
        grid=(n_tiles + 1,),
        in_specs=[
            pl.BlockSpec((1, t, d), lambda g: (0, 0, 0), pipeline_mode=pl.Buffered(1)),
            pl.BlockSpec((1, t, d), lambda g: (jnp.minimum(g + 1, n_tiles - 1), 0, 0)),
            pl.BlockSpec((1, t, d), lambda g: (tile_res(g), 0, 0)),
        ] + [_const_spec(a.shape) for a in consts],
        out_specs=pl.BlockSpec(memory_space=pl.ANY),
        out_shape=jax.ShapeDtypeStruct((bsz * seq // SEG, SEG, d), jnp.float32),
        scratch_shapes=[
            pltpu.VMEM((nc, t, 2 * cw), jnp.float32),
            pltpu.VMEM((nc, halo, 2 * cw), jnp.float32),
            pltpu.VMEM((t, d), jnp.bfloat16),
            pltpu.VMEM((t, d), jnp.bfloat16),
            pltpu.VMEM((t, d), jnp.float32),
            pltpu.VMEM((t, dff), jnp.bfloat16),
            pltpu.VMEM((2, t // V7X_SUBLANES, V7X_SUBLANES, d), jnp.float32),
            pltpu.SemaphoreType.DMA((2,)),
        ],
        compiler_params=pltpu.CompilerParams(
            dimension_semantics=("arbitrary",), vmem_limit_bytes=VMEM_LIMIT_BYTES),
        name="ffn",
    )(x_tiles, x_tiles, x_tiles, *consts)
    return out.reshape(bsz, seq, d)


def kernel(x, c, w_ada, b_ada, norm_mix_g, w_in, rnn_conv_w, rnn_conv_b, lru_w_a, lru_b_a, lru_w_x, lru_b_x, lru_lambda, sgu_ln_g, sgu_ln_b, sgu_w_s, sgu_b_s, w_branch_a, w_branch_b, w_out, norm_ffn_g, w_up, ffn_conv_w, ffn_conv_b, w_down, norm_final_g):
    depth = w_in.shape[0]
    d = x.shape[-1]
    hd = d // RNN_HEADS
    row = lambda v: v.reshape(1, -1)
    time_of = _block_time_index()
    for l in range(depth):
        mod3 = _adaln(c, w_ada[l], b_ada[l]).reshape(x.shape[0], 1, N_MOD * d)
        pa = lru_w_a[l].reshape(RNN_HEADS // 2, 2, hd, hd)
        px = lru_w_x[l].reshape(RNN_HEADS // 2, 2, hd, hd)
        zero = jnp.zeros_like(pa[:, 0])
        wax = jnp.concatenate([
            jnp.concatenate([pa[:, 0], zero, px[:, 0], zero], axis=-1),
            jnp.concatenate([zero, pa[:, 1], zero, px[:, 1]], axis=-1)], axis=1)
        (wax,) = _pack_rows([wax.reshape(d, 4 * hd)], 4 * hd)
        wax = wax.reshape(RNN_HEADS // 2, hd, 4 * hd)
        (w_in_p,) = _pack_rows([w_in[l]], PACK_WIDE_COL_BLOCK)
        w_ba, w_bb, w_o = _pack_rows([w_branch_a[l], w_branch_b[l], w_out[l]], PACK_COL_BLOCK)
        w_s = sgu_w_s[l][:, time_of, :][:, :, time_of]
        sgu_bias = jnp.repeat(sgu_b_s[l].T, d // SGU_GROUPS, axis=1)[time_of, :]
        x1 = _mixer(x, mod3, row(norm_mix_g[l]), w_in_p, rnn_conv_w[l],
                    row(rnn_conv_b[l]), wax, row(lru_b_a[l]), row(lru_b_x[l]), row(lru_lambda[l]),
                    row(sgu_ln_g[l]), row(sgu_ln_b[l]), w_s, sgu_bias, w_ba, w_bb, w_o)
        x = _ffn(x1, mod3, row(norm_ffn_g[l]), w_up[l], ffn_conv_w[l],
                 ffn_conv_b[l], w_down[l], row(norm_final_g),
                 final_norm=(l == depth - 1), out_shape=x.shape)
    return x
```

```python
import functools

import jax
import jax.numpy as jnp
import numpy as np
from jax.experimental import pallas as pl
from jax.experimental.pallas import tpu as pltpu

EPS = 1e-6
LRU_C = 8.0
RNN_HEADS = 8
SGU_GROUPS = 8
SGU_BLOCK = 128
N_MOD = 6

V7X_SUBLANES = 8
TIME_TILE = 256
SEG = SGU_BLOCK // V7X_SUBLANES
FFN_CHUNKS = 12
FFN_ROW_PIECE = 64
FFN_DOWN_GROUP = 3
FFN_DOWN_AFTER = (4, 7, 11, 11)
ADALN_COL_BLOCK = 1536
PACK_COL_BLOCK = 512
VMEM_LIMIT_BYTES = 56 * 1024 * 1024

_GELU_C0 = 0.7978845608028654
_GELU_C1 = 0.044715


def _gelu(x):
    inner = x * (_GELU_C0 + (_GELU_C0 * _GELU_C1) * (x * x))
    hx = 0.5 * x
    return hx + hx * jnp.tanh(inner)


def _sigmoid(x):
    return 0.5 * jnp.tanh(0.5 * x) + 0.5


def _bdot(a, b):
    return jnp.dot(a, b, preferred_element_type=jnp.float32)


def _unpack_rows(w):
    return pltpu.bitcast(w, jnp.bfloat16)


def _mod_norm(x, gs, shift):
    r = jax.lax.rsqrt(jnp.mean(x * x, axis=-1, keepdims=True) + EPS)
    return (x * r) * gs + shift


def _block_time_index():
    rho = np.arange(SGU_BLOCK)
    return (rho % V7X_SUBLANES) * SEG + rho // V7X_SUBLANES


def _sublane_iota(cols):
    return jax.lax.broadcasted_iota(jnp.int32, (V7X_SUBLANES, cols), 0)


def _boundary_groups(blk_tail, prev_tail):
    sub = _sublane_iota(blk_tail.shape[1])
    groups = []
    for i in range(0, blk_tail.shape[0], V7X_SUBLANES):
        cur = pltpu.roll(blk_tail[i:i + V7X_SUBLANES], 1, 0)
        prv = pltpu.roll(prev_tail[i:i + V7X_SUBLANES], 1, 0)
        groups.append(jnp.where(sub == 0, prv, cur))
    return jnp.concatenate(groups, axis=0)


def _causal_conv_block(x_blk, prev_tail, taps, bias):
    halo = prev_tail.shape[0]
    n = x_blk.shape[0]
    ext = jnp.concatenate([_boundary_groups(x_blk[n - halo:], prev_tail), x_blk], axis=0)
    y = bias + ext[0:n] * taps[0]
    for k in range(1, len(taps)):
        y = y + ext[k * V7X_SUBLANES:k * V7X_SUBLANES + n] * taps[k]
    return y


def _sublane_scan(a, b, carry):
    sub = _sublane_iota(a.shape[1])
    b = b + jnp.where(sub == 0, a * carry, 0.0)
    k = 1
    while k < V7X_SUBLANES:
        am = jnp.where(sub < k, 0.0, a)
        b = b + am * pltpu.roll(b, k, 0)
        a = am * pltpu.roll(a, k, 0)
        k *= 2
    return b


def _linear_scan_block(a, u, carry):
    n_groups = a.shape[0] // V7X_SUBLANES
    grp = lambda v, g: v[g * V7X_SUBLANES:(g + 1) * V7X_SUBLANES]
    hs, ps = [grp(u, 0)], [grp(a, 0)]
    for g in range(1, n_groups):
        hs.append(grp(a, g) * hs[-1] + grp(u, g))
        ps.append(grp(a, g) * ps[-1])
    seg_end = _sublane_scan(ps[-1], hs[-1], carry)
    sub = _sublane_iota(a.shape[1])
    seg_in = jnp.where(sub == 0, carry, pltpu.roll(seg_end, 1, 0))
    h = jnp.concatenate([hh + pp * seg_in for hh, pp in zip(hs, ps)], axis=0)
    return h, seg_end[V7X_SUBLANES - 1:V7X_SUBLANES, :]


def _tile_row_copies(hbm_rows, vmem_tile, first_row, sem, to_vmem):
    copies = []
    first_seg = first_row // SEG
    for n in range(vmem_tile.shape[0] // SEG):
        for g in range(SEG):
            hbm = hbm_rows.at[pl.ds(first_seg + n * V7X_SUBLANES, V7X_SUBLANES), g, :]
            vmem = vmem_tile.at[n * SEG + g]
            src, dst = (hbm, vmem) if to_vmem else (vmem, hbm)
            copies.append(pltpu.make_async_copy(src, dst, sem))
    return copies


def _pack_bits(w):
    return pltpu.bitcast(w.astype(jnp.bfloat16), jnp.uint32)


def _stream_pack(jobs, stage, sems):
    def copies_of(i):
        slot = i % 2
        return [pltpu.make_async_copy(
            src, stage.at[slot, :, pl.ds(off, src.shape[1])], sems.at[slot])
            for src, off in jobs[i][0]]

    pending = copies_of(0)
    for cp in pending:
        cp.start()
    for i, (_, store) in enumerate(jobs):
        upcoming = copies_of(i + 1) if i + 1 < len(jobs) else []
        for cp in upcoming:
            cp.start()
        for cp in pending:
            cp.wait()
        store(_pack_bits(stage[i % 2]))
        pending = upcoming


def _col_block_jobs(w_hbm, dst):
    def store_at(c):
        def store(v):
            dst[:, c:c + PACK_COL_BLOCK] = v
        return store
    return [([(w_hbm.at[:, pl.ds(c, PACK_COL_BLOCK)], 0)], store_at(c))
            for c in range(0, w_hbm.shape[1], PACK_COL_BLOCK)]


def _adaln_kernel(ct_ref, w_ref, b_ref, o_ref):
    ct = ct_ref[...]
    act = ct * _sigmoid(ct)
    w = w_ref[...]
    rows = []
    for b in range(ct.shape[1]):
        rows.append(jnp.sum(act[:, b:b + 1] * w, axis=0, keepdims=True))
    o_ref[...] = jnp.concatenate(rows, axis=0) + b_ref[...]


def _adaln(c, w_ada, b_ada):
    bsz, d = c.shape
    n = w_ada.shape[1]
    bn = ADALN_COL_BLOCK
    return pl.pallas_call(
        _adaln_kernel,
        grid=(n // bn,),
        in_specs=[
            pl.BlockSpec((d, bsz), lambda j: (0, 0)),
            pl.BlockSpec((d, bn), lambda j: (0, j)),
            pl.BlockSpec((1, bn), lambda j: (0, j)),
        ],
        out_specs=pl.BlockSpec((bsz, bn), lambda j: (0, j)),
        out_shape=jax.ShapeDtypeStruct((bsz, n), jnp.float32),
        compiler_params=pltpu.CompilerParams(
            dimension_semantics=("arbitrary",), vmem_limit_bytes=VMEM_LIMIT_BYTES),
        name="adaln",
    )(c.T, w_ada, b_ada.reshape(1, n))


def _const_spec(shape):
    zeros = (0,) * len(shape)
    return pl.BlockSpec(shape, lambda g: zeros, pipeline_mode=pl.Buffered(1))


def _mixer_kernel(x_hbm, w_in_hbm, wax_hbm, wba_hbm, wbb_hbm, wo_hbm,
                  mod_ref, g_ref, cw_ref, cb_ref, ba_ref, bx_ref,
                  lam_ref, lng_ref, lnb_ref, ws_ref, sb_ref,
                  o_ref, xbuf, sems, z_buf, hb_buf, xr_tail, h_carry, merged_buf,
                  w_in_ref, wax_ref, wba_ref, wbb_ref, wo_ref, stage, wsems,
                  *, tiles_per_seq, n_tiles):
    t, d = o_ref.shape[1], o_ref.shape[2]
    hd = d // RNN_HEADS
    gd = d // SGU_GROUPS
    kw = cw_ref.shape[0]
    halo = (kw - 1) * V7X_SUBLANES
    n_blocks = t // SGU_BLOCK
    step = pl.program_id(0)
    tile = jnp.minimum(step, n_tiles - 1)
    slot = jax.lax.rem(tile, 3)
    slot_prev = jax.lax.rem(step + 2, 3)
    seq_start = jax.lax.rem(tile, tiles_per_seq) == 0

    def x_copies(tile_idx, slot_idx):
        return _tile_row_copies(x_hbm, xbuf.at[slot_idx], tile_idx * t, sems.at[slot_idx], True)

    @pl.when(step == 0)
    def _():
        for cp in x_copies(0, 0):
            cp.start()
        merged_buf[...] = jnp.zeros_like(merged_buf)
        xbuf[2] = jnp.zeros(xbuf.shape[1:], jnp.float32)
        _stream_pack(
            _col_block_jobs(w_in_hbm, w_in_ref) + _col_block_jobs(wax_hbm, wax_ref)
            + _col_block_jobs(wba_hbm, wba_ref) + _col_block_jobs(wbb_hbm, wbb_ref)
            + _col_block_jobs(wo_hbm, wo_ref), stage, wsems)

    @pl.when(step < n_tiles)
    def _():
        for cp in x_copies(step, slot):
            cp.wait()

    @pl.when(step + 1 < n_tiles)
    def _():
        for cp in x_copies(step + 1, jax.lax.rem(step + 1, 3)):
            cp.start()

    gate1_prev = mod_ref[jnp.maximum(step - 1, 0) // tiles_per_seq][:, 2 * d:3 * d]
    mo = _bdot(merged_buf[...], _unpack_rows(wo_ref[...]))
    o_ref[0] = xbuf[slot_prev].reshape(t, d) + gate1_prev * mo

    mod = mod_ref[tile // tiles_per_seq]
    shift1, scale1 = mod[:, 0:d], mod[:, d:2 * d]
    x = xbuf[slot].reshape(t, d)
    hb_buf[...] = _mod_norm(x, g_ref[...] * (1.0 + scale1), shift1).astype(jnp.bfloat16)

    def in_proj(j):
        z_buf[j] = _bdot(hb_buf[...], _unpack_rows(w_in_ref[:, j * d:(j + 1) * d]))

    in_proj(0)
    in_proj(3)
    in_proj(2)

    taps = [cw_ref[k:k + 1, :] for k in range(kw)]
    prev_tail = jnp.where(seq_start, 0.0, xr_tail[...])
    xc_blocks = []
    for n in range(n_blocks):
        blk = z_buf[0, n * SGU_BLOCK:(n + 1) * SGU_BLOCK, :]
        xc_blocks.append(_causal_conv_block(blk, prev_tail, taps, cb_ref[...]))
        prev_tail = blk[SGU_BLOCK - halo:]
    xr_tail[...] = prev_tail
    xc = jnp.concatenate(xc_blocks, axis=0)
    xcb = xc.astype(jnp.bfloat16)
    pre = [_bdot(xcb[:, i * 2 * hd:(i + 1) * 2 * hd],
                 _unpack_rows(wax_ref[i * hd:(i + 1) * hd, :]))
           for i in range(RNN_HEADS // 2)]

    in_proj(1)
    in_proj(4)
    in_proj(5)

    gi = _sigmoid(jnp.concatenate([p[:, 2 * hd:] for p in pre], axis=1) + bx_ref[...])
    lam = lam_ref[...]
    log_sig = jnp.minimum(lam, 0.0) - jnp.log1p(jnp.exp(-jnp.abs(lam)))
    half_c = (0.5 * LRU_C) * log_sig
    pre_r = jnp.concatenate([p[:, :2 * hd] for p in pre], axis=1)
    log_a = jnp.tanh(0.5 * pre_r + 0.5 * ba_ref[...]) * half_c + half_c
    a = jnp.exp(log_a)
    m2 = -jnp.tanh(log_a) * (a * a + 1.0)
    mult = jnp.where(m2 > 0.0, m2 * jax.lax.rsqrt(m2), 0.0)
    u = mult * (gi * xc)
    state = jnp.where(seq_start, 0.0, h_carry[...])
    hs_blocks = []
    for n in range(n_blocks):
        rows = slice(n * SGU_BLOCK, (n + 1) * SGU_BLOCK)
        hs, state = _linear_scan_block(a[rows], u[rows], state)
        hs_blocks.append(hs)
    h_carry[...] = state
    ya_pre = (jnp.concatenate(hs_blocks, axis=0) * _gelu(z_buf[1])).astype(jnp.bfloat16)

    gu = _gelu(z_buf[2])
    gv = _gelu(z_buf[3])
    mu = jnp.mean(gv, axis=-1, keepdims=True)
    gc = gv - mu
    var = jnp.mean(gc * gc, axis=-1, keepdims=True)
    vb = ((gc * jax.lax.rsqrt(var + EPS)) * lng_ref[...] + lnb_ref[...]).astype(jnp.bfloat16)
    ri = jax.lax.broadcasted_iota(jnp.int32, (SGU_BLOCK, SGU_BLOCK), 0)
    ci = jax.lax.broadcasted_iota(jnp.int32, (SGU_BLOCK, SGU_BLOCK), 1)
    time_of = lambda i: (i & (V7X_SUBLANES - 1)) * SEG + (i >> 3)
    causal = time_of(ci) <= time_of(ri)
    wm = [jnp.where(causal, ws_ref[g], 0.0).astype(jnp.bfloat16) for g in range(SGU_GROUPS)]
    mixed = []
    for g in range(SGU_GROUPS):
        v_g = jnp.concatenate([vb[n * SGU_BLOCK:(n + 1) * SGU_BLOCK, g * gd:(g + 1) * gd]
                               for n in range(n_blocks)], axis=1)
        mixed.append(_bdot(wm[g], v_g))
    blocks = [jnp.concatenate([m[:, n * gd:(n + 1) * gd] for m in mixed], axis=1) + sb_ref[...]
              for n in range(n_blocks)]
    yb_pre = (gu * jnp.concatenate(blocks, axis=0)).astype(jnp.bfloat16)

    yb = _bdot(yb_pre, _unpack_rows(wbb_ref[...]))
    ya = _bdot(ya_pre, _unpack_rows(wba_ref[...]))
    merged = _sigmoid(z_buf[4]) * ya + _sigmoid(z_buf[5]) * yb
    merged_buf[...] = merged.astype(jnp.bfloat16)


def _mixer(x, mod3, norm_g, w_in, conv_w, conv_b, wax, b_a, b_x, lam, ln_g, ln_b, w_s, sgu_bias,
           w_ba, w_bb, w_out):
    bsz, seq, d = x.shape
    t = TIME_TILE
    n_t = seq // t
    n_tiles = bsz * n_t
    weights = [w_in, wax, w_ba, w_bb, w_out]
    consts = [mod3, norm_g, conv_w, conv_b, b_a, b_x, lam, ln_g, ln_b, w_s, sgu_bias]
    halo = (conv_w.shape[0] - 1) * V7X_SUBLANES
    packed = lambda w: pltpu.VMEM((w.shape[0] // 2, w.shape[1]), jnp.uint32)
    return pl.pallas_call(
        functools.partial(_mixer_kernel, tiles_per_seq=n_t, n_tiles=n_tiles),
        grid=(n_tiles + 1,),
        in_specs=[pl.BlockSpec(memory_space=pl.ANY)] * (1 + len(weights))
        + [_const_spec(a.shape) for a in consts],
        out_specs=pl.BlockSpec((1, t, d), lambda g: (jnp.maximum(g - 1, 0), 0, 0)),
        out_shape=jax.ShapeDtypeStruct((n_tiles, t, d), jnp.float32),
        scratch_shapes=[
            pltpu.VMEM((3, t // V7X_SUBLANES, V7X_SUBLANES, d), jnp.float32),
            pltpu.SemaphoreType.DMA((3,)),
            pltpu.VMEM((6, t, d), jnp.float32),
            pltpu.VMEM((t, d), jnp.bfloat16),
            pltpu.VMEM((halo, d), jnp.float32),
            pltpu.VMEM((1, d), jnp.float32),
            pltpu.VMEM((t, d), jnp.bfloat16),
        ] + [packed(w) for w in weights] + [
            pltpu.VMEM((2, w_in.shape[0], PACK_COL_BLOCK), jnp.float32),
            pltpu.SemaphoreType.DMA((2,)),
        ],
        compiler_params=pltpu.CompilerParams(
            dimension_semantics=("arbitrary",), vmem_limit_bytes=VMEM_LIMIT_BYTES),
        name="mixer",
    )(x.reshape(bsz * seq // SEG, SEG, d), *weights, *consts)


def _ffn_kernel(x0_ref, xnext_ref, xres_ref, wup_hbm, wdn_hbm, mod_ref, g_ref, cw_ref, cb_ref,
                gf_ref, out_hbm, up_buf, tail_buf, hb_cur, hb_next, acc_buf, p_buf,
                obuf, sems, wup_ref, wdn_ref, stage, wsems,
                *, final_norm, tiles_per_seq, n_tiles, n_chunks):
    t = xnext_ref.shape[1]
    d = xnext_ref.shape[2]
    cw = wdn_ref.shape[0] * 2 // n_chunks
    kw = cw_ref.shape[0]
    halo = (kw - 1) * V7X_SUBLANES
    n_blocks = t // SGU_BLOCK
    step = pl.program_id(0)
    seq_start = jax.lax.rem(step, tiles_per_seq) == 0
    g_norm = g_ref[...]
    oslot = jax.lax.rem(step + 1, 2)

    def out_copies(tile_idx, slot_idx):
        return _tile_row_copies(out_hbm, obuf.at[slot_idx], tile_idx * t, sems.at[slot_idx], False)

    def norm_rows(x, b):
        mod = mod_ref[b]
        return _mod_norm(x, g_norm * (1.0 + mod[:, 4 * d:5 * d]),
                         mod[:, 3 * d:4 * d]).astype(jnp.bfloat16)

    @pl.when(step == 0)
    def _():
        hb_next[...] = norm_rows(x0_ref[0], 0)
        acc_buf[...] = jnp.zeros_like(acc_buf)
        dff = wdn_hbm.shape[0]

        def store_block(dst, r, c):
            def store(v):
                dst[r:r + v.shape[0], c:c + v.shape[1]] = v
            return store

        jobs = [([(wup_hbm.at[:, pl.ds(k * cw, cw)], 0),
                  (wup_hbm.at[:, pl.ds(dff + k * cw, cw)], cw)],
                 store_block(wup_ref, 0, k * 2 * cw)) for k in range(n_chunks)]
        jobs += [([(wdn_hbm.at[pl.ds(r, d), pl.ds(c, PACK_COL_BLOCK)], 0)],
                  store_block(wdn_ref, r // 2, c))
                 for r in range(0, dff, d) for c in range(0, d, PACK_COL_BLOCK)]
        _stream_pack(jobs, stage, wsems)

    @pl.when(step >= 3)
    def _():
        for cp in out_copies(step - 3, oslot):
            cp.wait()

    hb_cur[...] = hb_next[...]

    b_res = jnp.maximum(step - 1, 0) // tiles_per_seq
    b_next = jnp.minimum(step + 1, n_tiles - 1) // tiles_per_seq
    gate2 = mod_ref[b_res][:, 5 * d:6 * d]
    for r in range(0, t, FFN_ROW_PIECE):
        rows = slice(r, r + FFN_ROW_PIECE)
        y = xres_ref[0, rows, :] + gate2 * acc_buf[rows, :]
        if final_norm:
            y = (y * jax.lax.rsqrt(jnp.mean(y * y, axis=-1, keepdims=True) + EPS)) * gf_ref[...]
        groups = slice(r // V7X_SUBLANES, (r + FFN_ROW_PIECE) // V7X_SUBLANES)
        obuf[oslot, groups] = y.reshape(FFN_ROW_PIECE // V7X_SUBLANES, V7X_SUBLANES, d)
        hb_next[rows, :] = norm_rows(xnext_ref[0, rows, :], b_next)

    def down_group(g):
        lo, hi = g * FFN_DOWN_GROUP * cw, (g + 1) * FFN_DOWN_GROUP * cw
        part = _bdot(p_buf[:, lo:hi], _unpack_rows(wdn_ref[lo // 2:hi // 2, :]))
        if g == 0:
            acc_buf[...] = part
        else:
            acc_buf[...] += part

    for k in range(n_chunks):
        cols = slice(k * 2 * cw, (k + 1) * 2 * cw)
        up_buf[k] = _bdot(hb_cur[...], _unpack_rows(wup_ref[:, cols]))
        taps = [cw_ref[j:j + 1, cols] for j in range(kw)]
        prev_tail = jnp.where(seq_start, 0.0, tail_buf[k])
        y_blocks = []
        for n in range(n_blocks):
            blk = up_buf[k, n * SGU_BLOCK:(n + 1) * SGU_BLOCK, :]
            y_blocks.append(_causal_conv_block(blk, prev_tail, taps, cb_ref[:, cols]))
            prev_tail = blk[SGU_BLOCK - halo:]
        tail_buf[k] = prev_tail
        y = jnp.concatenate(y_blocks, axis=0)
        p_buf[:, k * cw:(k + 1) * cw] = (_gelu(y[:, :cw]) * y[:, cw:]).astype(jnp.bfloat16)
        for g, after in enumerate(FFN_DOWN_AFTER):
            if after == k:
                down_group(g)

    @pl.when(step >= 1)
    def _():
        for cp in out_copies(step - 1, oslot):
            cp.start()

    @pl.when(step == n_tiles)
    def _():
        for cp in out_copies(step - 2, 1 - oslot):
            cp.wait()
        for cp in out_copies(step - 1, oslot):
            cp.wait()


def _ffn(x_tiles, mod3, norm_g, w_up, conv_w, conv_b, w_down, norm_final_g, final_norm, out_shape):
    n_tiles, t, d = x_tiles.shape
    bsz, seq, _ = out_shape
    dff = w_down.shape[0]
    nc = FFN_CHUNKS
    cw = dff // nc
    n_t = seq // t
    halo = (conv_w.shape[0] - 1) * V7X_SUBLANES
    perm = lambda a: a.reshape(-1, 2, nc, cw).transpose(0, 2, 1, 3).reshape(-1, 2 * dff)
    assert 2 * cw == PACK_COL_BLOCK
    consts = [mod3, norm_g, perm(conv_w), perm(conv_b), norm_final_g]
    tile_res = lambda g: jnp.maximum(g - 1, 0)
    out = pl.pallas_call(
        functools.partial(_ffn_kernel, final_norm=final_norm, tiles_per_seq=n_t,
                          n_tiles=n_tiles, n_chunks=nc),
        grid=(n_tiles + 1,),
        in_specs=[
            pl.BlockSpec((1, t, d), lambda g: (0, 0, 0), pipeline_mode=pl.Buffered(1)),
            pl.BlockSpec((1, t, d), lambda g: (jnp.minimum(g + 1, n_tiles - 1), 0, 0)),
            pl.BlockSpec((1, t, d), lambda g: (tile_res(g), 0, 0)),
            pl.BlockSpec(memory_space=pl.ANY),
            pl.BlockSpec(memory_space=pl.ANY),
        ] + [_const_spec(a.shape) for a in consts],
        out_specs=pl.BlockSpec(memory_space=pl.ANY),
        out_shape=jax.ShapeDtypeStruct((bsz * seq // SEG, SEG, d), jnp.float32),
        scratch_shapes=[
            pltpu.VMEM((nc, t, 2 * cw), jnp.float32),
            pltpu.VMEM((nc, halo, 2 * cw), jnp.float32),
            pltpu.VMEM((t, d), jnp.bfloat16),
            pltpu.VMEM((t, d), jnp.bfloat16),
            pltpu.VMEM((t, d), jnp.float32),
            pltpu.VMEM((t, dff), jnp.bfloat16),
            pltpu.VMEM((2, t // V7X_SUBLANES, V7X_SUBLANES, d), jnp.float32),
            pltpu.SemaphoreType.DMA((2,)),
            pltpu.VMEM((d // 2, 2 * dff), jnp.uint32),
            pltpu.VMEM((dff // 2, d), jnp.uint32),
            pltpu.VMEM((2, d, PACK_COL_BLOCK), jnp.float32),
            pltpu.SemaphoreType.DMA((2,)),
        ],
        compiler_params=pltpu.CompilerParams(
            dimension_semantics=("arbitrary",), vmem_limit_bytes=VMEM_LIMIT_BYTES),
        name="ffn",
    )(x_tiles, x_tiles, x_tiles, w_up, w_down, *consts)
    return out.reshape(bsz, seq, d)


def kernel(x, c, w_ada, b_ada, norm_mix_g, w_in, rnn_conv_w, rnn_conv_b, lru_w_a, lru_b_a, lru_w_x, lru_b_x, lru_lambda, sgu_ln_g, sgu_ln_b, sgu_w_s, sgu_b_s, w_branch_a, w_branch_b, w_out, norm_ffn_g, w_up, ffn_conv_w, ffn_conv_b, w_down, norm_final_g):
    depth = w_in.shape[0]
    d = x.shape[-1]
    hd = d // RNN_HEADS
    row = lambda v: v.reshape(1, -1)
    time_of = _block_time_index()
    for l in range(depth):
        mod3 = _adaln(c, w_ada[l], b_ada[l]).reshape(x.shape[0], 1, N_MOD * d)
        pa = lru_w_a[l].reshape(RNN_HEADS // 2, 2, hd, hd)
        px = lru_w_x[l].reshape(RNN_HEADS // 2, 2, hd, hd)
        zero = jnp.zeros_like(pa[:, 0])
        wax = jnp.concatenate([
            jnp.concatenate([pa[:, 0], zero, px[:, 0], zero], axis=-1),
            jnp.concatenate([zero, pa[:, 1], zero, px[:, 1]], axis=-1)], axis=1)
        wax = wax.reshape(d, 4 * hd)
        w_s = sgu_w_s[l][:, time_of, :][:, :, time_of]
        sgu_bias = jnp.repeat(sgu_b_s[l].T, d // SGU_GROUPS, axis=1)[time_of, :]
        x1 = _mixer(x, mod3, row(norm_mix_g[l]), w_in[l], rnn_conv_w[l],
                    row(rnn_conv_b[l]), wax, row(lru_b_a[l]), row(lru_b_x[l]), row(lru_lambda[l]),
                    row(sgu_ln_g[l]), row(sgu_ln_b[l]), w_s, sgu_bias,
                    w_branch_a[l], w_branch_b[l], w_out[l])
        x = _ffn(x1, mod3, row(norm_ffn_g[l]), w_up[l], ffn_conv_w[l],
                 ffn_conv_b[l], w_down[l], row(norm_final_g),
                 final_norm=(l == depth - 1), out_shape=x.shape)
    return x
```

```python
import functools

import jax
import jax.numpy as jnp
import numpy as np
from jax.experimental import pallas as pl
from jax.experimental.pallas import tpu as pltpu

EPS = 1e-6
LRU_C = 8.0
RNN_HEADS = 8
SGU_GROUPS = 8
SGU_BLOCK = 128
N_MOD = 6

V7X_SUBLANES = 8
TIME_TILE = 256
SEG = SGU_BLOCK // V7X_SUBLANES
FFN_CHUNKS = 12
FFN_ROW_PIECE = 64
FFN_DOWN_GROUP = 3
FFN_DOWN_AFTER = (4, 7, 11, 11)
ADALN_COL_BLOCK = 1536
PACK_COL_BLOCK = 512
VMEM_LIMIT_BYTES = 56 * 1024 * 1024

_GELU_C0 = 0.7978845608028654
_GELU_C1 = 0.044715


def _gelu(x):
    inner = x * (_GELU_C0 + (_GELU_C0 * _GELU_C1) * (x * x))
    hx = 0.5 * x
    return hx + hx * jnp.tanh(inner)


def _sigmoid(x):
    return 0.5 * jnp.tanh(0.5 * x) + 0.5


def _bdot(a, b):
    return jnp.dot(a, b, preferred_element_type=jnp.float32)


def _unpack_rows(w):
    return pltpu.bitcast(w, jnp.bfloat16)


def _mod_norm(x, gs, shift):
    r = jax.lax.rsqrt(jnp.mean(x * x, axis=-1, keepdims=True) + EPS)
    return (x * r) * gs + shift


def _block_time_index():
    rho = np.arange(SGU_BLOCK)
    return (rho % V7X_SUBLANES) * SEG + rho // V7X_SUBLANES


def _sublane_iota(cols):
    return jax.lax.broadcasted_iota(jnp.int32, (V7X_SUBLANES, cols), 0)


def _boundary_groups(blk_tail, prev_tail):
    sub = _sublane_iota(blk_tail.shape[1])
    groups = []
    for i in range(0, blk_tail.shape[0], V7X_SUBLANES):
        cur = pltpu.roll(blk_tail[i:i + V7X_SUBLANES], 1, 0)
        prv = pltpu.roll(prev_tail[i:i + V7X_SUBLANES], 1, 0)
        groups.append(jnp.where(sub == 0, prv, cur))
    return jnp.concatenate(groups, axis=0)


def _causal_conv_block(x_blk, prev_tail, taps, bias):
    halo = prev_tail.shape[0]
    n = x_blk.shape[0]
    ext = jnp.concatenate([_boundary_groups(x_blk[n - halo:], prev_tail), x_blk], axis=0)
    y = bias + ext[0:n] * taps[0]
    for k in range(1, len(taps)):
        y = y + ext[k * V7X_SUBLANES:k * V7X_SUBLANES + n] * taps[k]
    return y


def _sublane_scan(a, b, carry):
    sub = _sublane_iota(a.shape[1])
    b = b + jnp.where(sub == 0, a * carry, 0.0)
    k = 1
    while k < V7X_SUBLANES:
        am = jnp.where(sub < k, 0.0, a)
        b = b + am * pltpu.roll(b, k, 0)
        a = am * pltpu.roll(a, k, 0)
        k *= 2
    return b


def _linear_scan_block(a, u, carry):
    n_groups = a.shape[0] // V7X_SUBLANES
    grp = lambda v, g: v[g * V7X_SUBLANES:(g + 1) * V7X_SUBLANES]
    hs, ps = [grp(u, 0)], [grp(a, 0)]
    for g in range(1, n_groups):
        hs.append(grp(a, g) * hs[-1] + grp(u, g))
        ps.append(grp(a, g) * ps[-1])
    seg_end = _sublane_scan(ps[-1], hs[-1], carry)
    sub = _sublane_iota(a.shape[1])
    seg_in = jnp.where(sub == 0, carry, pltpu.roll(seg_end, 1, 0))
    h = jnp.concatenate([hh + pp * seg_in for hh, pp in zip(hs, ps)], axis=0)
    return h, seg_end[V7X_SUBLANES - 1:V7X_SUBLANES, :]


def _tile_row_copies(hbm_rows, vmem_tile, first_row, sem, to_vmem):
    copies = []
    first_seg = first_row // SEG
    for n in range(vmem_tile.shape[0] // SEG):
        for g in range(SEG):
            hbm = hbm_rows.at[pl.ds(first_seg + n * V7X_SUBLANES, V7X_SUBLANES), g, :]
            vmem = vmem_tile.at[n * SEG + g]
            src, dst = (hbm, vmem) if to_vmem else (vmem, hbm)
            copies.append(pltpu.make_async_copy(src, dst, sem))
    return copies


def _pack_bits(w):
    return pltpu.bitcast(w.astype(jnp.bfloat16), jnp.uint32)


def _stream_pack(jobs, stage, sems):
    def copies_of(i):
        slot = i % 2
        return [pltpu.make_async_copy(
            src, stage.at[slot, :, pl.ds(off, src.shape[1])], sems.at[slot])
            for src, off in jobs[i][0]]

    pending = copies_of(0)
    for cp in pending:
        cp.start()
    for i, (_, store) in enumerate(jobs):
        upcoming = copies_of(i + 1) if i + 1 < len(jobs) else []
        for cp in upcoming:
            cp.start()
        for cp in pending:
            cp.wait()
        store(_pack_bits(stage[i % 2]))
        pending = upcoming


def _col_block_jobs(w_hbm, dst):
    def store_at(c):
        def store(v):
            dst[:, c:c + PACK_COL_BLOCK] = v
        return store
    return [([(w_hbm.at[:, pl.ds(c, PACK_COL_BLOCK)], 0)], store_at(c))
            for c in range(0, w_hbm.shape[1], PACK_COL_BLOCK)]


def _adaln_kernel(ct_ref, w_ref, b_ref, o_ref):
    ct = ct_ref[...]
    act = ct * _sigmoid(ct)
    w = w_ref[...]
    rows = []
    for b in range(ct.shape[1]):
        rows.append(jnp.sum(act[:, b:b + 1] * w, axis=0, keepdims=True))
    o_ref[...] = jnp.concatenate(rows, axis=0) + b_ref[...]


def _adaln(c, w_ada, b_ada):
    bsz, d = c.shape
    n = w_ada.shape[1]
    bn = ADALN_COL_BLOCK
    return pl.pallas_call(
        _adaln_kernel,
        grid=(n // bn,),
        in_specs=[
            pl.BlockSpec((d, bsz), lambda j: (0, 0)),
            pl.BlockSpec((d, bn), lambda j: (0, j)),
            pl.BlockSpec((1, bn), lambda j: (0, j)),
        ],
        out_specs=pl.BlockSpec((bsz, bn), lambda j: (0, j)),
        out_shape=jax.ShapeDtypeStruct((bsz, n), jnp.float32),
        compiler_params=pltpu.CompilerParams(
            dimension_semantics=("arbitrary",), vmem_limit_bytes=VMEM_LIMIT_BYTES),
        name="adaln",
    )(c.T, w_ada, b_ada.reshape(1, n))


def _const_spec(shape):
    zeros = (0,) * len(shape)
    return pl.BlockSpec(shape, lambda g: zeros, pipeline_mode=pl.Buffered(1))


def _mixer_kernel(x_hbm, w_in_hbm, wax_hbm, wba_hbm, wbb_hbm, wo_hbm,
                  mod_ref, g_ref, cw_ref, cb_ref, ba_ref, bx_ref,
                  lam_ref, lng_ref, lnb_ref, ws_ref, sb_ref,
                  out_hbm, xbuf, sems, z_buf, hb_buf, xr_tail, h_carry, merged_buf,
                  w_in_ref, wax_ref, wba_ref, wbb_ref, wo_ref, stage, wsems, obuf, osems,
                  *, tiles_per_seq, n_tiles):
    t, d = obuf.shape[1], obuf.shape[2]
    hd = d // RNN_HEADS
    gd = d // SGU_GROUPS
    kw = cw_ref.shape[0]
    halo = (kw - 1) * V7X_SUBLANES
    n_blocks = t // SGU_BLOCK
    step = pl.program_id(0)
    tile = step
    slot = jax.lax.rem(tile, 3)
    slot_prev = jax.lax.rem(step + 2, 3)
    oslot = jax.lax.rem(step + 1, 2)
    seq_start = jax.lax.rem(tile, tiles_per_seq) == 0

    def x_copies(tile_idx, slot_idx):
        return _tile_row_copies(x_hbm, xbuf.at[slot_idx], tile_idx * t, sems.at[slot_idx], True)

    def out_copy(tile_idx, slot_idx):
        return pltpu.make_async_copy(obuf.at[slot_idx], out_hbm.at[tile_idx], osems.at[slot_idx])

    def out_proj(x_slot, b, slot_idx):
        mo = _bdot(merged_buf[...], _unpack_rows(wo_ref[...]))
        obuf[slot_idx] = xbuf[x_slot].reshape(t, d) + mod_ref[b][:, 2 * d:3 * d] * mo

    @pl.when(step == 0)
    def _():
        for cp in x_copies(0, 0):
            cp.start()
        merged_buf[...] = jnp.zeros_like(merged_buf)
        xbuf[2] = jnp.zeros(xbuf.shape[1:], jnp.float32)
        _stream_pack(
            _col_block_jobs(w_in_hbm, w_in_ref) + _col_block_jobs(wax_hbm, wax_ref)
            + _col_block_jobs(wba_hbm, wba_ref) + _col_block_jobs(wbb_hbm, wbb_ref)
            + _col_block_jobs(wo_hbm, wo_ref), stage, wsems)

    for cp in x_copies(step, slot):
        cp.wait()

    @pl.when(step + 1 < n_tiles)
    def _():
        for cp in x_copies(step + 1, jax.lax.rem(step + 1, 3)):
            cp.start()

    @pl.when(step >= 3)
    def _():
        out_copy(step - 3, oslot).wait()

    out_proj(slot_prev, jnp.maximum(step - 1, 0) // tiles_per_seq, oslot)

    mod = mod_ref[tile // tiles_per_seq]
    shift1, scale1 = mod[:, 0:d], mod[:, d:2 * d]
    x = xbuf[slot].reshape(t, d)
    hb_buf[...] = _mod_norm(x, g_ref[...] * (1.0 + scale1), shift1).astype(jnp.bfloat16)

    def in_proj(j):
        z_buf[j] = _bdot(hb_buf[...], _unpack_rows(w_in_ref[:, j * d:(j + 1) * d]))

    in_proj(0)
    in_proj(3)
    in_proj(2)

    taps = [cw_ref[k:k + 1, :] for k in range(kw)]
    prev_tail = jnp.where(seq_start, 0.0, xr_tail[...])
    xc_blocks = []
    for n in range(n_blocks):
        blk = z_buf[0, n * SGU_BLOCK:(n + 1) * SGU_BLOCK, :]
        xc_blocks.append(_causal_conv_block(blk, prev_tail, taps, cb_ref[...]))
        prev_tail = blk[SGU_BLOCK - halo:]
    xr_tail[...] = prev_tail
    xc = jnp.concatenate(xc_blocks, axis=0)
    xcb = xc.astype(jnp.bfloat16)
    pre = [_bdot(xcb[:, i * 2 * hd:(i + 1) * 2 * hd],
                 _unpack_rows(wax_ref[i * hd:(i + 1) * hd, :]))
           for i in range(RNN_HEADS // 2)]

    in_proj(1)
    in_proj(4)
    in_proj(5)

    gi = _sigmoid(jnp.concatenate([p[:, 2 * hd:] for p in pre], axis=1) + bx_ref[...])
    lam = lam_ref[...]
    log_sig = jnp.minimum(lam, 0.0) - jnp.log1p(jnp.exp(-jnp.abs(lam)))
    half_c = (0.5 * LRU_C) * log_sig
    pre_r = jnp.concatenate([p[:, :2 * hd] for p in pre], axis=1)
    log_a = jnp.tanh(0.5 * pre_r + 0.5 * ba_ref[...]) * half_c + half_c
    a = jnp.exp(log_a)
    m2 = -jnp.tanh(log_a) * (a * a + 1.0)
    mult = jnp.where(m2 > 0.0, m2 * jax.lax.rsqrt(m2), 0.0)
    u = mult * (gi * xc)
    state = jnp.where(seq_start, 0.0, h_carry[...])
    hs_blocks = []
    for n in range(n_blocks):
        rows = slice(n * SGU_BLOCK, (n + 1) * SGU_BLOCK)
        hs, state = _linear_scan_block(a[rows], u[rows], state)
        hs_blocks.append(hs)
    h_carry[...] = state
    ya_pre = (jnp.concatenate(hs_blocks, axis=0) * _gelu(z_buf[1])).astype(jnp.bfloat16)

    gu = _gelu(z_buf[2])
    gv = _gelu(z_buf[3])
    mu = jnp.mean(gv, axis=-1, keepdims=True)
    gc = gv - mu
    var = jnp.mean(gc * gc, axis=-1, keepdims=True)
    vb = ((gc * jax.lax.rsqrt(var + EPS)) * lng_ref[...] + lnb_ref[...]).astype(jnp.bfloat16)
    ri = jax.lax.broadcasted_iota(jnp.int32, (SGU_BLOCK, SGU_BLOCK), 0)
    ci = jax.lax.broadcasted_iota(jnp.int32, (SGU_BLOCK, SGU_BLOCK), 1)
    time_of = lambda i: (i & (V7X_SUBLANES - 1)) * SEG + (i >> 3)
    causal = time_of(ci) <= time_of(ri)
    wm = [jnp.where(causal, ws_ref[g], 0.0).astype(jnp.bfloat16) for g in range(SGU_GROUPS)]
    mixed = []
    for g in range(SGU_GROUPS):
        v_g = jnp.concatenate([vb[n * SGU_BLOCK:(n + 1) * SGU_BLOCK, g * gd:(g + 1) * gd]
                               for n in range(n_blocks)], axis=1)
        mixed.append(_bdot(wm[g], v_g))
    blocks = [jnp.concatenate([m[:, n * gd:(n + 1) * gd] for m in mixed], axis=1) + sb_ref[...]
              for n in range(n_blocks)]
    yb_pre = (gu * jnp.concatenate(blocks, axis=0)).astype(jnp.bfloat16)

    yb = _bdot(yb_pre, _unpack_rows(wbb_ref[...]))
    ya = _bdot(ya_pre, _unpack_rows(wba_ref[...]))
    merged = _sigmoid(z_buf[4]) * ya + _sigmoid(z_buf[5]) * yb
    merged_buf[...] = merged.astype(jnp.bfloat16)

    @pl.when(step >= 1)
    def _():
        out_copy(step - 1, oslot).start()

    @pl.when(step == n_tiles - 1)
    def _():
        last_slot = 1 - oslot
        out_copy(step - 2, last_slot).wait()
        out_proj(slot, step // tiles_per_seq, last_slot)
        out_copy(step, last_slot).start()
        out_copy(step - 1, oslot).wait()
        out_copy(step, last_slot).wait()


def _mixer(x, mod3, norm_g, w_in, conv_w, conv_b, wax, b_a, b_x, lam, ln_g, ln_b, w_s, sgu_bias,
           w_ba, w_bb, w_out):
    bsz, seq, d = x.shape
    t = TIME_TILE
    n_t = seq // t
    n_tiles = bsz * n_t
    weights = [w_in, wax, w_ba, w_bb, w_out]
    consts = [mod3, norm_g, conv_w, conv_b, b_a, b_x, lam, ln_g, ln_b, w_s, sgu_bias]
    halo = (conv_w.shape[0] - 1) * V7X_SUBLANES
    packed = lambda w: pltpu.VMEM((w.shape[0] // 2, w.shape[1]), jnp.uint32)
    return pl.pallas_call(
        functools.partial(_mixer_kernel, tiles_per_seq=n_t, n_tiles=n_tiles),
        grid=(n_tiles,),
        in_specs=[pl.BlockSpec(memory_space=pl.ANY)] * (1 + len(weights))
        + [_const_spec(a.shape) for a in consts],
        out_specs=pl.BlockSpec(memory_space=pl.ANY),
        out_shape=jax.ShapeDtypeStruct((n_tiles, t, d), jnp.float32),
        scratch_shapes=[
            pltpu.VMEM((3, t // V7X_SUBLANES, V7X_SUBLANES, d), jnp.float32),
            pltpu.SemaphoreType.DMA((3,)),
            pltpu.VMEM((6, t, d), jnp.float32),
            pltpu.VMEM((t, d), jnp.bfloat16),
            pltpu.VMEM((halo, d), jnp.float32),
            pltpu.VMEM((1, d), jnp.float32),
            pltpu.VMEM((t, d), jnp.bfloat16),
        ] + [packed(w) for w in weights] + [
            pltpu.VMEM((2, w_in.shape[0], PACK_COL_BLOCK), jnp.float32),
            pltpu.SemaphoreType.DMA((2,)),
            pltpu.VMEM((2, t, d), jnp.float32),
            pltpu.SemaphoreType.DMA((2,)),
        ],
        compiler_params=pltpu.CompilerParams(
            dimension_semantics=("arbitrary",), vmem_limit_bytes=VMEM_LIMIT_BYTES),
        name="mixer",
    )(x.reshape(bsz * seq // SEG, SEG, d), *weights, *consts)


def _ffn_kernel(x0_ref, xnext_ref, xres_ref, wup_hbm, wdn_hbm, mod_ref, g_ref, cw_ref, cb_ref,
                gf_ref, out_hbm, up_buf, tail_buf, hb_cur, hb_next, acc_buf, p_buf,
                obuf, sems, wup_ref, wdn_ref, stage, wsems,
                *, final_norm, tiles_per_seq, n_tiles, n_chunks):
    t = xnext_ref.shape[1]
    d = xnext_ref.shape[2]
    cw = wdn_ref.shape[0] * 2 // n_chunks
    kw = cw_ref.shape[0]
    halo = (kw - 1) * V7X_SUBLANES
    n_blocks = t // SGU_BLOCK
    step = pl.program_id(0)
    seq_start = jax.lax.rem(step, tiles_per_seq) == 0
    g_norm = g_ref[...]
    oslot = jax.lax.rem(step + 1, 2)

    def out_copies(tile_idx, slot_idx):
        return _tile_row_copies(out_hbm, obuf.at[slot_idx], tile_idx * t, sems.at[slot_idx], False)

    def norm_rows(x, b):
        mod = mod_ref[b]
        return _mod_norm(x, g_norm * (1.0 + mod[:, 4 * d:5 * d]),
                         mod[:, 3 * d:4 * d]).astype(jnp.bfloat16)

    @pl.when(step == 0)
    def _():
        hb_next[...] = norm_rows(x0_ref[0], 0)
        acc_buf[...] = jnp.zeros_like(acc_buf)
        dff = wdn_hbm.shape[0]

        def store_block(dst, r, c):
            def store(v):
                dst[r:r + v.shape[0], c:c + v.shape[1]] = v
            return store

        jobs = [([(wup_hbm.at[:, pl.ds(k * cw, cw)], 0),
                  (wup_hbm.at[:, pl.ds(dff + k * cw, cw)], cw)],
                 store_block(wup_ref, 0, k * 2 * cw)) for k in range(n_chunks)]
        jobs += [([(wdn_hbm.at[pl.ds(r, d), pl.ds(c, PACK_COL_BLOCK)], 0)],
                  store_block(wdn_ref, r // 2, c))
                 for r in range(0, dff, d) for c in range(0, d, PACK_COL_BLOCK)]
        _stream_pack(jobs, stage, wsems)

    @pl.when(step >= 3)
    def _():
        for cp in out_copies(step - 3, oslot):
            cp.wait()

    hb_cur[...] = hb_next[...]

    def residual_rows(x_ref, r, b, slot_idx):
        rows = slice(r, r + FFN_ROW_PIECE)
        y = x_ref[0, rows, :] + mod_ref[b][:, 5 * d:6 * d] * acc_buf[rows, :]
        if final_norm:
            y = (y * jax.lax.rsqrt(jnp.mean(y * y, axis=-1, keepdims=True) + EPS)) * gf_ref[...]
        groups = slice(r // V7X_SUBLANES, (r + FFN_ROW_PIECE) // V7X_SUBLANES)
        obuf[slot_idx, groups] = y.reshape(FFN_ROW_PIECE // V7X_SUBLANES, V7X_SUBLANES, d)

    b_res = jnp.maximum(step - 1, 0) // tiles_per_seq
    b_next = jnp.minimum(step + 1, n_tiles - 1) // tiles_per_seq
    for r in range(0, t, FFN_ROW_PIECE):
        residual_rows(xres_ref, r, b_res, oslot)
        rows = slice(r, r + FFN_ROW_PIECE)
        hb_next[rows, :] = norm_rows(xnext_ref[0, rows, :], b_next)

    def down_group(g):
        lo, hi = g * FFN_DOWN_GROUP * cw, (g + 1) * FFN_DOWN_GROUP * cw
        part = _bdot(p_buf[:, lo:hi], _unpack_rows(wdn_ref[lo // 2:hi // 2, :]))
        if g == 0:
            acc_buf[...] = part
        else:
            acc_buf[...] += part

    for k in range(n_chunks):
        cols = slice(k * 2 * cw, (k + 1) * 2 * cw)
        up_buf[k] = _bdot(hb_cur[...], _unpack_rows(wup_ref[:, cols]))
        taps = [cw_ref[j:j + 1, cols] for j in range(kw)]
        prev_tail = jnp.where(seq_start, 0.0, tail_buf[k])
        y_blocks = []
        for n in range(n_blocks):
            blk = up_buf[k, n * SGU_BLOCK:(n + 1) * SGU_BLOCK, :]
            y_blocks.append(_causal_conv_block(blk, prev_tail, taps, cb_ref[:, cols]))
            prev_tail = blk[SGU_BLOCK - halo:]
        tail_buf[k] = prev_tail
        y = jnp.concatenate(y_blocks, axis=0)
        p_buf[:, k * cw:(k + 1) * cw] = (_gelu(y[:, :cw]) * y[:, cw:]).astype(jnp.bfloat16)
        for g, after in enumerate(FFN_DOWN_AFTER):
            if after == k:
                down_group(g)

    @pl.when(step >= 1)
    def _():
        for cp in out_copies(step - 1, oslot):
            cp.start()

    @pl.when(step == n_tiles - 1)
    def _():
        last_slot = 1 - oslot
        for cp in out_copies(step - 2, last_slot):
            cp.wait()
        for r in range(0, t, FFN_ROW_PIECE):
            residual_rows(xnext_ref, r, step // tiles_per_seq, last_slot)
        for cp in out_copies(step, last_slot):
            cp.start()
        for cp in out_copies(step - 1, oslot):
            cp.wait()
        for cp in out_copies(step, last_slot):
            cp.wait()


def _ffn(x_tiles, mod3, norm_g, w_up, conv_w, conv_b, w_down, norm_final_g, final_norm, out_shape):
    n_tiles, t, d = x_tiles.shape
    bsz, seq, _ = out_shape
    dff = w_down.shape[0]
    nc = FFN_CHUNKS
    cw = dff // nc
    n_t = seq // t
    halo = (conv_w.shape[0] - 1) * V7X_SUBLANES
    perm = lambda a: a.reshape(-1, 2, nc, cw).transpose(0, 2, 1, 3).reshape(-1, 2 * dff)
    assert 2 * cw == PACK_COL_BLOCK and n_tiles >= 3
    consts = [mod3, norm_g, perm(conv_w), perm(conv_b), norm_final_g]
    tile_res = lambda g: jnp.maximum(g - 1, 0)
    out = pl.pallas_call(
        functools.partial(_ffn_kernel, final_norm=final_norm, tiles_per_seq=n_t,
                          n_tiles=n_tiles, n_chunks=nc),
        grid=(n_tiles,),
        in_specs=[
            pl.BlockSpec((1, t, d), lambda g: (0, 0, 0), pipeline_mode=pl.Buffered(1)),
            pl.BlockSpec((1, t, d), lambda g: (jnp.minimum(g + 1, n_tiles - 1), 0, 0)),
            pl.BlockSpec((1, t, d), lambda g: (tile_res(g), 0, 0)),
            pl.BlockSpec(memory_space=pl.ANY),
            pl.BlockSpec(memory_space=pl.ANY),
        ] + [_const_spec(a.shape) for a in consts],
        out_specs=pl.BlockSpec(memory_space=pl.ANY),
        out_shape=jax.ShapeDtypeStruct((bsz * seq // SEG, SEG, d), jnp.float32),
        scratch_shapes=[
            pltpu.VMEM((nc, t, 2 * cw), jnp.float32),
            pltpu.VMEM((nc, halo, 2 * cw), jnp.float32),
            pltpu.VMEM((t, d), jnp.bfloat16),
            pltpu.VMEM((t, d), jnp.bfloat16),
            pltpu.VMEM((t, d), jnp.float32),
            pltpu.VMEM((t, dff), jnp.bfloat16),
            pltpu.VMEM((2, t // V7X_SUBLANES, V7X_SUBLANES, d), jnp.float32),
            pltpu.SemaphoreType.DMA((2,)),
            pltpu.VMEM((d // 2, 2 * dff), jnp.uint32),
            pltpu.VMEM((dff // 2, d), jnp.uint32),
            pltpu.VMEM((2, d, PACK_COL_BLOCK), jnp.float32),
            pltpu.SemaphoreType.DMA((2,)),
        ],
        compiler_params=pltpu.CompilerParams(
            dimension_semantics=("arbitrary",), vmem_limit_bytes=VMEM_LIMIT_BYTES),
        name="ffn",
    )(x_tiles, x_tiles, x_tiles, w_up, w_down, *consts)
    return out.reshape(bsz, seq, d)


def kernel(x, c, w_ada, b_ada, norm_mix_g, w_in, rnn_conv_w, rnn_conv_b, lru_w_a, lru_b_a, lru_w_x, lru_b_x, lru_lambda, sgu_ln_g, sgu_ln_b, sgu_w_s, sgu_b_s, w_branch_a, w_branch_b, w_out, norm_ffn_g, w_up, ffn_conv_w, ffn_conv_b, w_down, norm_final_g):
    depth = w_in.shape[0]
    d = x.shape[-1]
    hd = d // RNN_HEADS
    row = lambda v: v.reshape(1, -1)
    time_of = _block_time_index()
    for l in range(depth):
        mod3 = _adaln(c, w_ada[l], b_ada[l]).reshape(x.shape[0], 1, N_MOD * d)
        pa = lru_w_a[l].reshape(RNN_HEADS // 2, 2, hd, hd)
        px = lru_w_x[l].reshape(RNN_HEADS // 2, 2, hd, hd)
        zero = jnp.zeros_like(pa[:, 0])
        wax = jnp.concatenate([
            jnp.concatenate([pa[:, 0], zero, px[:, 0], zero], axis=-1),
            jnp.concatenate([zero, pa[:, 1], zero, px[:, 1]], axis=-1)], axis=1)
        wax = wax.reshape(d, 4 * hd)
        w_s = sgu_w_s[l][:, time_of, :][:, :, time_of]
        sgu_bias = jnp.repeat(sgu_b_s[l].T, d // SGU_GROUPS, axis=1)[time_of, :]
        x1 = _mixer(x, mod3, row(norm_mix_g[l]), w_in[l], rnn_conv_w[l],
                    row(rnn_conv_b[l]), wax, row(lru_b_a[l]), row(lru_b_x[l]), row(lru_lambda[l]),
                    row(sgu_ln_g[l]), row(sgu_ln_b[l]), w_s, sgu_bias,
                    w_branch_a[l], w_branch_b[l], w_out[l])
        x = _ffn(x1, mod3, row(norm_ffn_g[l]), w_up[l], ffn_conv_w[l],
                 ffn_conv_b[l], w_down[l], row(norm_final_g),
                 final_norm=(l == depth - 1), out_shape=x.shape)
    return x
```
